```python
import jax, jax.numpy as jnp
from jax import lax
import numpy as np

D_MODEL = 2048
BATCH = 1
SEQ = 8192
DEPTH = 4

GRID_W = 64
CTX_LEN = 256
N_MIXERS = 2
HEAD_DIM = 128
N_HEADS = D_MODEL // HEAD_DIM
N_KV_HEADS = N_HEADS // 4
GROUP = N_HEADS // N_KV_HEADS
D_Q = N_HEADS * HEAD_DIM
D_KV = N_KV_HEADS * HEAD_DIM
WINDOW = 128
BLOCK = 128
ROPE_PAIRS = HEAD_DIM // 4
ROPE_BASE = 10000.0
CONV_W = 3
D_FF = ((8 * D_MODEL // 3 + 255) // 256) * 256
N_ATTN_LAYERS = (DEPTH + N_MIXERS - 1) // N_MIXERS
N_CONV_LAYERS = DEPTH // N_MIXERS
EPS = 1e-6
NEG_INF = -1e30

kernel_name = "hybrid_swa_shortconv_dit_prefix"


def rms_norm(x):
    xf = x.astype(jnp.float32)
    return (xf * lax.rsqrt(jnp.mean(xf * xf, axis=-1, keepdims=True) + EPS)).astype(x.dtype)


def modulate(x, shift, scale):
    return rms_norm(x) * (1 + scale) + shift


def adaln(cond_act, w, b):
    return jnp.split(cond_act @ w + b, 6, axis=-1)


def dwconv3(x, w):
    xp = jnp.pad(x, ((0, 0), (1, 1), (0, 0)))
    return xp[:, :-2] * w[0] + xp[:, 1:-1] * w[1] + xp[:, 2:] * w[2]


def rope_tables(n):
    rows = n // GRID_W
    row = jnp.repeat(jnp.arange(rows), GRID_W).astype(jnp.float32)
    col = jnp.tile(jnp.arange(GRID_W), rows).astype(jnp.float32)
    inv = ROPE_BASE ** (-jnp.arange(ROPE_PAIRS, dtype=jnp.float32) / ROPE_PAIRS)
    ang = jnp.stack([row[:, None] * inv, col[:, None] * inv], axis=1)
    ang = jnp.broadcast_to(ang[:, :, None, :], (n, 2, 2, ROPE_PAIRS)).reshape(n, HEAD_DIM)
    return jnp.cos(ang), jnp.sin(ang)


def apply_rope(x, cos, sin):
    xf = x.astype(jnp.float32)
    xr = xf.reshape(*x.shape[:-1], 2, 2, ROPE_PAIRS)
    rot = jnp.stack([-xr[..., 1, :], xr[..., 0, :]], axis=-2).reshape(x.shape)
    return (xf * cos[:, None, :] + rot * sin[:, None, :]).astype(x.dtype)


def band_mask(n):
    nb = n // BLOCK
    qi = jnp.arange(BLOCK)[None, :, None]
    kk = jnp.arange(3 * BLOCK)[None, None, :]
    blk = jnp.arange(nb)[:, None, None]
    kpos = blk * BLOCK - BLOCK + kk
    rel = qi - kk + BLOCK
    return (jnp.abs(rel) <= WINDOW) & (kpos >= 0) & (kpos < n)


def sink_softmax(s, sink):
    m = jnp.maximum(jnp.max(s, axis=-1, keepdims=True), sink)
    e = jnp.exp(s - m)
    return e / (jnp.sum(e, axis=-1, keepdims=True) + jnp.exp(sink - m))


def project_q(h, w_q, gain):
    q = (h @ w_q).reshape(*h.shape[:-1], N_HEADS, HEAD_DIM)
    return rms_norm(q) * gain


def project_kv(h, w_kv, gain):
    k, v = jnp.split(h @ w_kv, 2, axis=-1)
    k = rms_norm(k.reshape(*h.shape[:-1], N_KV_HEADS, HEAD_DIM)) * gain
    v = v.reshape(*h.shape[:-1], N_KV_HEADS, HEAD_DIM)
    return k, v


def attention_mixer(h_x, h_c, w_qkv, w_o, q_gain, k_gain, sink, cos, sin, mask, need_ctx):
    b, n, _ = h_x.shape
    nb = n // BLOCK
    scale = HEAD_DIM ** -0.5
    w_q, w_kv = w_qkv[:, :D_Q], w_qkv[:, D_Q:]
    q = apply_rope(project_q(h_x, w_q, q_gain), cos, sin)
    k, v = project_kv(h_x, w_kv, k_gain)
    k = apply_rope(k, cos, sin)
    kc, vc = project_kv(h_c, w_kv, k_gain)
    sink_f = sink.astype(jnp.float32).reshape(N_KV_HEADS, GROUP)

    qb = q.reshape(b, nb, BLOCK, N_KV_HEADS, GROUP, HEAD_DIM)
    pad = ((0, 0), (BLOCK, BLOCK), (0, 0), (0, 0))
    kp = jnp.pad(k, pad).reshape(b, nb + 2, BLOCK, N_KV_HEADS, HEAD_DIM)
    vp = jnp.pad(v, pad).reshape(b, nb + 2, BLOCK, N_KV_HEADS, HEAD_DIM)
    kb = jnp.concatenate([kp[:, :-2], kp[:, 1:-1], kp[:, 2:]], axis=2)
    vb = jnp.concatenate([vp[:, :-2], vp[:, 1:-1], vp[:, 2:]], axis=2)
    s_loc = jnp.einsum('bnqhgd,bnkhd->bhgnqk', qb, kb).astype(jnp.float32) * scale
    s_loc = jnp.where(mask, s_loc, NEG_INF)
    s_ctx = jnp.einsum('bnqhgd,bkhd->bhgnqk', qb, kc).astype(jnp.float32) * scale
    p = sink_softmax(jnp.concatenate([s_loc, s_ctx], axis=-1),
                     sink_f[None, :, :, None, None, None]).astype(v.dtype)
    o = (jnp.einsum('bhgnqk,bnkhd->bnqhgd', p[..., :3 * BLOCK], vb)
         + jnp.einsum('bhgnqk,bkhd->bnqhgd', p[..., 3 * BLOCK:], vc))
    out_x = o.reshape(b, n, D_Q) @ w_o

    out_c = None
    if need_ctx:
        l = h_c.shape[1]
        qc = project_q(h_c, w_q, q_gain).reshape(b, l, N_KV_HEADS, GROUP, HEAD_DIM)
        s_c = jnp.einsum('bqhgd,bkhd->bhgqk', qc, kc).astype(jnp.float32) * scale
        p_c = sink_softmax(s_c, sink_f[None, :, :, None, None]).astype(vc.dtype)
        oc = jnp.einsum('bhgqk,bkhd->bqhgd', p_c, vc)
        out_c = oc.reshape(b, l, D_Q) @ w_o
    return out_x, out_c


def short_conv_mixer(h, w_in, conv_w, w_out):
    gate_b, gate_c, val = jnp.split(h @ w_in, 3, axis=-1)
    return (gate_b * dwconv3(gate_c * val, conv_w)) @ w_out


def conv_ffn(h, w_up, conv_w, conv_b, w_down):
    gate, val = jnp.split(h @ w_up, 2, axis=-1)
    gate = dwconv3(gate, conv_w) + conv_b
    return (jax.nn.silu(gate) * val) @ w_down


def setup_inputs(seed: int = 0) -> dict:
    key = jax.random.key(seed)
    ks = jax.random.split(key, 20)

    def nrm(k, shape, scale):
        return jax.random.normal(k, shape, jnp.float32) * scale

    d = D_MODEL
    return {
        "x": nrm(ks[0], (BATCH, SEQ, d), 1.0),
        "c": nrm(ks[1], (BATCH, d), 1.0),
        "ctx": nrm(ks[2], (BATCH, CTX_LEN, d), 1.0),
        "c_ctx": nrm(ks[3], (d,), 1.0),
        "w_ada": nrm(ks[4], (DEPTH, d, 6 * d), 0.5 * d ** -0.5),
        "b_ada": nrm(ks[5], (DEPTH, 6 * d), 0.02),
        "attn_w_qkv": nrm(ks[6], (N_ATTN_LAYERS, d, D_Q + 2 * D_KV), d ** -0.5),
        "attn_w_o": nrm(ks[7], (N_ATTN_LAYERS, D_Q, d), D_Q ** -0.5),
        "attn_q_gain": 1.0 + nrm(ks[8], (N_ATTN_LAYERS, HEAD_DIM), 0.1),
        "attn_k_gain": 1.0 + nrm(ks[9], (N_ATTN_LAYERS, HEAD_DIM), 0.1),
        "attn_sink": nrm(ks[10], (N_ATTN_LAYERS, N_HEADS), 0.5),
        "sc_w_in": nrm(ks[11], (N_CONV_LAYERS, d, 3 * d), d ** -0.5),
        "sc_conv": nrm(ks[12], (N_CONV_LAYERS, CONV_W, d), CONV_W ** -0.5),
        "sc_w_out": nrm(ks[13], (N_CONV_LAYERS, d, d), d ** -0.5),
        "ffn_w_up": nrm(ks[14], (DEPTH, d, 2 * D_FF), d ** -0.5),
        "ffn_conv": nrm(ks[15], (DEPTH, CONV_W, D_FF), CONV_W ** -0.5),
        "ffn_conv_b": nrm(ks[16], (DEPTH, D_FF), 0.02),
        "ffn_w_down": nrm(ks[17], (DEPTH, D_FF, d), D_FF ** -0.5),
    }


def reference(x, c, ctx, c_ctx, w_ada, b_ada, attn_w_qkv, attn_w_o, attn_q_gain, attn_k_gain,
              attn_sink, sc_w_in, sc_conv, sc_w_out, ffn_w_up, ffn_conv, ffn_conv_b, ffn_w_down):
    n = x.shape[1]
    cos, sin = rope_tables(n)
    mask = band_mask(n)
    silu_c = jax.nn.silu(c)
    silu_cc = jax.nn.silu(c_ctx)

    for l in range(DEPTH):
        is_attn = (l % N_MIXERS) == 0
        j = l // N_MIXERS
        need_ctx = l < DEPTH - 1
        sh_m, sc_m, g_m, sh_f, sc_f, g_f = [t[:, None, :] for t in adaln(silu_c, w_ada[l], b_ada[l])]
        h_x = modulate(x, sh_m, sc_m)
        y_c = None
        if need_ctx or is_attn:
            csh_m, csc_m, cg_m, csh_f, csc_f, cg_f = adaln(silu_cc, w_ada[l], b_ada[l])
            h_c = modulate(ctx, csh_m, csc_m)
        if is_attn:
            y_x, y_c = attention_mixer(h_x, h_c, attn_w_qkv[j], attn_w_o[j], attn_q_gain[j],
                                       attn_k_gain[j], attn_sink[j], cos, sin, mask, need_ctx)
        else:
            y_x = short_conv_mixer(h_x, sc_w_in[j], sc_conv[j], sc_w_out[j])
            if need_ctx:
                y_c = short_conv_mixer(h_c, sc_w_in[j], sc_conv[j], sc_w_out[j])
        x = x + g_m * y_x
        x = x + g_f * conv_ffn(modulate(x, sh_f, sc_f), ffn_w_up[l], ffn_conv[l], ffn_conv_b[l], ffn_w_down[l])
        if need_ctx:
            ctx = ctx + cg_m * y_c
            ctx = ctx + cg_f * conv_ffn(modulate(ctx, csh_f, csc_f), ffn_w_up[l], ffn_conv[l],
                                        ffn_conv_b[l], ffn_w_down[l])
    return x
```

```python
import functools

import jax
import jax.numpy as jnp
from jax import lax
from jax.experimental import pallas as pl
from jax.experimental.pallas import tpu as pltpu

HEAD_DIM = 128
N_KV_HEADS = 4
GROUP = 4
WINDOW = 128
GRID_W = 64
ROPE_PAIRS = HEAD_DIM // 4
ROPE_BASE = 10000.0
N_MIXERS = 2
EPS = 1e-6
NEG_INF = -1e30

BF16 = jnp.bfloat16
F32 = jnp.float32

V7X_VMEM_LIMIT_BYTES = 56 * 1024 * 1024
HALO_ROWS = 16
MOD_ROWS = 8
ROW_LATENT = 0
ROW_CTX = 1


def _params(*sem):
    return pltpu.CompilerParams(dimension_semantics=sem, vmem_limit_bytes=V7X_VMEM_LIMIT_BYTES)


def _dot(a, b):
    return jnp.dot(a, b, preferred_element_type=F32)


def _ada_kernel(cc_ref, w_ref, b_ref, o_ref):
    a = jax.nn.silu(cc_ref[...]).astype(BF16)
    o_ref[...] = _dot(a, w_ref[...].astype(BF16)) + b_ref[...]


def _ada_table(cc, w_ada, b_ada, tn=1024):
    depth, d, d6 = w_ada.shape
    return pl.pallas_call(
        _ada_kernel,
        grid=(depth, d6 // tn),
        in_specs=[
            pl.BlockSpec((MOD_ROWS, d), lambda l, j: (0, 0)),
            pl.BlockSpec((None, d, tn), lambda l, j: (l, 0, j)),
            pl.BlockSpec((None, 1, tn), lambda l, j: (l, 0, j)),
        ],
        out_specs=pl.BlockSpec((None, MOD_ROWS, tn), lambda l, j: (l, 0, j)),
        out_shape=jax.ShapeDtypeStruct((depth, MOD_ROWS, d6), F32),
        compiler_params=_params("arbitrary", "arbitrary"),
        name="ada_table",
    )(cc, w_ada, b_ada.reshape(depth, 1, d6))


def _modulate_into(x_ref, sh_ref, sc_ref, h_ref, row):
    x = x_ref[...]
    r = lax.rsqrt(jnp.mean(x * x, axis=-1, keepdims=True) + EPS)
    h = (x * r) * (1 + sc_ref[row:row + 1, :]) + sh_ref[row:row + 1, :]
    h_ref[...] = h.astype(BF16)


def _mod_spec(layer, chunk, d):
    return pl.BlockSpec((None, MOD_ROWS, d), lambda *_: (layer, 0, chunk))


def _conv3(g, prev_row, next_row, w_ref):
    tm = g.shape[0]
    rows = lax.broadcasted_iota(jnp.int32, g.shape, 0)
    dn = jnp.where(rows == 0, prev_row, pltpu.roll(g, 1, 0))
    up = jnp.where(rows == tm - 1, next_row, pltpu.roll(g, tm - 1, 0))
    return dn * w_ref[0:1, :] + g * w_ref[1:2, :] + up * w_ref[2:3, :]


def _halo_rows(prev_ref, next_ref):
    i = pl.program_id(0)
    last = pl.num_programs(0) - 1
    prev_row = jnp.where(i > 0, prev_ref[HALO_ROWS - 1:HALO_ROWS, :].astype(F32), 0.0)
    next_row = jnp.where(i < last, next_ref[0:1, :].astype(F32), 0.0)
    return prev_row, next_row


def _halo_specs(tm, tk, m):
    per = tm // HALO_ROWS
    nblk = m // HALO_ROWS
    prev = pl.BlockSpec((HALO_ROWS, tk), lambda i, k: (jnp.maximum(i * per - 1, 0), k))
    nxt = pl.BlockSpec((HALO_ROWS, tk), lambda i, k: (jnp.minimum((i + 1) * per, nblk - 1), k))
    return prev, nxt


def _accumulate_residual(a, w_ref, x_ref, g_ref, o_ref, row):
    k = pl.program_id(1)
    part = _dot(a, w_ref[...].astype(BF16))

    @pl.when(k == 0)
    def _():
        o_ref[...] = part

    @pl.when(k > 0)
    def _():
        o_ref[...] += part

    @pl.when(k == pl.num_programs(1) - 1)
    def _():
        o_ref[...] = x_ref[...] + g_ref[row:row + 1, :] * o_ref[...]


def _qkv_kernel(*refs, row, rope, heads_per_tile, n_q_tiles, n_qk_tiles):
    if rope:
        x_ref, sh_ref, sc_ref, w_ref, gain_ref, cos_ref, sin_ref, o_ref, h_ref = refs
    else:
        x_ref, sh_ref, sc_ref, w_ref, gain_ref, o_ref, h_ref = refs
    j = pl.program_id(1)

    @pl.when(j == 0)
    def _():
        _modulate_into(x_ref, sh_ref, sc_ref, h_ref, row)

    y = _dot(h_ref[...], w_ref[...].astype(BF16))

    @pl.when(j < n_qk_tiles)
    def _():
        gain = jnp.where(j < n_q_tiles, gain_ref[0:1, :], gain_ref[1:2, :])
        if rope:
            cos = cos_ref[...]
            sin = sin_ref[...]
            lane = lax.broadcasted_iota(jnp.int32, cos.shape, 1)
            first_half = (lane & ROPE_PAIRS) == 0
        for hh in range(heads_per_tile):
            sl = slice(hh * HEAD_DIM, (hh + 1) * HEAD_DIM)
            yh = y[:, sl]
            r = lax.rsqrt(jnp.mean(yh * yh, axis=-1, keepdims=True) + EPS)
            yh = (yh * r) * gain
            if rope:
                rot = jnp.where(first_half, pltpu.roll(yh, HEAD_DIM - ROPE_PAIRS, 1), pltpu.roll(yh, ROPE_PAIRS, 1))
                yh = yh * cos + rot * sin
            o_ref[:, sl] = yh.astype(BF16)

    @pl.when(j >= n_qk_tiles)
    def _():
        o_ref[...] = y.astype(BF16)


def _qkv_proj(x, mod, layer, w_qkv, jl, gains, rope_tabs, row, tm, tn=512):
    m, d = x.shape
    n_out = w_qkv.shape[2]
    d_q = d
    d_kv = (n_out - d_q) // 2
    rope = rope_tabs is not None
    in_specs = [
        pl.BlockSpec((tm, d), lambda i, j: (i, 0)),
        _mod_spec(layer, 0, d),
        _mod_spec(layer, 1, d),
        pl.BlockSpec((None, d, tn), lambda i, j: (jl, 0, j)),
        pl.BlockSpec((2, HEAD_DIM), lambda i, j: (0, 0)),
    ]
    args = [x, mod, mod, w_qkv, gains]
    if rope:
        in_specs += [pl.BlockSpec((tm, HEAD_DIM), lambda i, j: (i, 0))] * 2
        args += list(rope_tabs)
    kern = functools.partial(_qkv_kernel, row=row, rope=rope, heads_per_tile=tn // HEAD_DIM,
                             n_q_tiles=d_q // tn, n_qk_tiles=(d_q + d_kv) // tn)
    return pl.pallas_call(
        kern,
        grid=(m // tm, n_out // tn),
        in_specs=in_specs,
        out_specs=pl.BlockSpec((tm, tn), lambda i, j: (i, j)),
        out_shape=jax.ShapeDtypeStruct((m, n_out), BF16),
        scratch_shapes=[pltpu.VMEM((tm, d), BF16)],
        compiler_params=_params("arbitrary", "arbitrary"),
        name="qkv_proj",
    )(*args)


def _sink_softmax_pv(q, k, v, valid, sink):
    s = lax.dot_general(q, k, (((1,), (1,)), ((), ())), preferred_element_type=F32) * (HEAD_DIM ** -0.5)
    if valid is not None:
        s = jnp.where(valid, s, NEG_INF)
    m = jnp.maximum(jnp.max(s, axis=-1, keepdims=True), sink)
    e = jnp.exp(s - m)
    den = jnp.sum(e, axis=-1, keepdims=True) + jnp.exp(sink - m)
    return _dot(e.astype(BF16), v) / den


def _band_attn_kernel(sink_ref, q_ref, kp_ref, km_ref, kn_ref, kc_ref, vp_ref, vm_ref, vn_ref, vc_ref, o_ref, *, n):
    h = pl.program_id(0)
    i = pl.program_id(1)
    tq = q_ref.shape[0]
    n_loc = tq + 2 * WINDOW
    n_keys = n_loc + kc_ref.shape[0]
    k = jnp.concatenate([kp_ref[...], km_ref[...], kn_ref[...], kc_ref[...]], axis=0)
    v = jnp.concatenate([vp_ref[...], vm_ref[...], vn_ref[...], vc_ref[...]], axis=0)
    row = lax.broadcasted_iota(jnp.int32, (tq, n_keys), 0)
    col = lax.broadcasted_iota(jnp.int32, (tq, n_keys), 1)
    rel = col - row
    kpos = i * tq - WINDOW + col
    valid = (col >= n_loc) | ((rel >= 0) & (rel <= 2 * WINDOW) & (kpos >= 0) & (kpos < n))
    for g in range(GROUP):
        sl = slice(g * HEAD_DIM, (g + 1) * HEAD_DIM)
        o = _sink_softmax_pv(q_ref[:, sl], k, v, valid, sink_ref[h * GROUP + g])
        o_ref[:, sl] = o.astype(BF16)


def _band_attention(qkv, qkv_c, sink, tq=256):
    n = qkv.shape[0]
    l_ctx = qkv_c.shape[0]
    d_q = N_KV_HEADS * GROUP * HEAD_DIM
    kcol = d_q // HEAD_DIM
    vcol = kcol + N_KV_HEADS
    per = tq // WINDOW
    nblk = n // WINDOW

    def prev(col0):
        return pl.BlockSpec((WINDOW, HEAD_DIM), lambda h, i: (jnp.maximum(i * per - 1, 0), col0 + h))

    def main(col0):
        return pl.BlockSpec((tq, HEAD_DIM), lambda h, i: (i, col0 + h))

    def nxt(col0):
        return pl.BlockSpec((WINDOW, HEAD_DIM), lambda h, i: (jnp.minimum((i + 1) * per, nblk - 1), col0 + h))

    def ctx(col0):
        return pl.BlockSpec((l_ctx, HEAD_DIM), lambda h, i: (0, col0 + h))

    return pl.pallas_call(
        functools.partial(_band_attn_kernel, n=n),
        grid=(N_KV_HEADS, n // tq),
        in_specs=[
            pl.BlockSpec(memory_space=pltpu.SMEM),
            pl.BlockSpec((tq, GROUP * HEAD_DIM), lambda h, i: (i, h)),
            prev(kcol), main(kcol), nxt(kcol), ctx(kcol),
            prev(vcol), main(vcol), nxt(vcol), ctx(vcol),
        ],
        out_specs=pl.BlockSpec((tq, GROUP * HEAD_DIM), lambda h, i: (i, h)),
        out_shape=jax.ShapeDtypeStruct((n, d_q), BF16),
        compiler_params=_params("arbitrary", "arbitrary"),
        name="band_attention",
    )(sink, qkv, qkv, qkv, qkv, qkv_c, qkv, qkv, qkv, qkv_c)


def _ctx_attn_kernel(sink_ref, q_ref, k_ref, v_ref, o_ref):
    h = pl.program_id(0)
    for g in range(GROUP):
        sl = slice(g * HEAD_DIM, (g + 1) * HEAD_DIM)
        o = _sink_softmax_pv(q_ref[:, sl], k_ref[...], v_ref[...], None, sink_ref[h * GROUP + g])
        o_ref[:, sl] = o.astype(BF16)


def _ctx_attention(qkv_c, sink):
    l_ctx = qkv_c.shape[0]
    d_q = N_KV_HEADS * GROUP * HEAD_DIM
    kcol = d_q // HEAD_DIM
    vcol = kcol + N_KV_HEADS
    return pl.pallas_call(
        _ctx_attn_kernel,
        grid=(N_KV_HEADS,),
        in_specs=[
            pl.BlockSpec(memory_space=pltpu.SMEM),
            pl.BlockSpec((l_ctx, GROUP * HEAD_DIM), lambda h: (0, h)),
            pl.BlockSpec((l_ctx, HEAD_DIM), lambda h: (0, kcol + h)),
            pl.BlockSpec((l_ctx, HEAD_DIM), lambda h: (0, vcol + h)),
        ],
        out_specs=pl.BlockSpec((l_ctx, GROUP * HEAD_DIM), lambda h: (0, h)),
        out_shape=jax.ShapeDtypeStruct((l_ctx, d_q), BF16),
        compiler_params=_params("arbitrary"),
        name="ctx_attention",
    )(sink, qkv_c, qkv_c, qkv_c)


def _attn_out_kernel(a_ref, w_ref, x_ref, g_ref, o_ref, *, row):
    _accumulate_residual(a_ref[...], w_ref, x_ref, g_ref, o_ref, row)


def _attn_out(a, w_o, jl, x, mod, layer, row, tm, tk=512):
    m, d = x.shape
    kdim = a.shape[1]
    return pl.pallas_call(
        functools.partial(_attn_out_kernel, row=row),
        grid=(m // tm, kdim // tk),
        in_specs=[
            pl.BlockSpec((tm, tk), lambda i, k: (i, k)),
            pl.BlockSpec((None, tk, d), lambda i, k: (jl, k, 0)),
            pl.BlockSpec((tm, d), lambda i, k: (i, 0)),
            _mod_spec(layer, 2, d),
        ],
        out_specs=pl.BlockSpec((tm, d), lambda i, k: (i, 0)),
        out_shape=jax.ShapeDtypeStruct((m, d), F32),
        compiler_params=_params("arbitrary", "arbitrary"),
        name="attn_out",
    )(a, w_o, x, mod)


def _conv_in_kernel(x_ref, sh_ref, sc_ref, wb_ref, wc_ref, wv_ref, gb_ref, p_ref, h_ref, *, row):
    @pl.when(pl.program_id(1) == 0)
    def _():
        _modulate_into(x_ref, sh_ref, sc_ref, h_ref, row)

    h = h_ref[...]
    gb_ref[...] = _dot(h, wb_ref[...].astype(BF16)).astype(BF16)
    gate_c = _dot(h, wc_ref[...].astype(BF16))
    val = _dot(h, wv_ref[...].astype(BF16))
    p_ref[...] = (gate_c * val).astype(BF16)


def _conv_in(x, mod, layer, w_in, jl, row, tm, tn=256):
    m, d = x.shape
    nt = d // tn

    def wspec(part):
        return pl.BlockSpec((None, d, tn), lambda i, j: (jl, 0, part * nt + j))

    out = jax.ShapeDtypeStruct((m, d), BF16)
    return pl.pallas_call(
        functools.partial(_conv_in_kernel, row=row),
        grid=(m // tm, nt),
        in_specs=[
            pl.BlockSpec((tm, d), lambda i, j: (i, 0)),
            _mod_spec(layer, 0, d),
            _mod_spec(layer, 1, d),
            wspec(0), wspec(1), wspec(2),
        ],
        out_specs=[pl.BlockSpec((tm, tn), lambda i, j: (i, j))] * 2,
        out_shape=[out, out],
        scratch_shapes=[pltpu.VMEM((tm, d), BF16)],
        compiler_params=_params("arbitrary", "arbitrary"),
        name="conv_in",
    )(x, mod, mod, w_in, w_in, w_in)


def _conv_out_kernel(gb_ref, p_ref, pp_ref, pn_ref, cw_ref, w_ref, x_ref, g_ref, o_ref, *, row):
    prev_row, next_row = _halo_rows(pp_ref, pn_ref)
    a = gb_ref[...].astype(F32) * _conv3(p_ref[...].astype(F32), prev_row, next_row, cw_ref)
    _accumulate_residual(a.astype(BF16), w_ref, x_ref, g_ref, o_ref, row)


def _conv_out(gb, p, conv_w, w_out, jl, x, mod, layer, row, tm, tk=512):
    m, d = x.shape
    prev, nxt = _halo_specs(tm, tk, m)
    return pl.pallas_call(
        functools.partial(_conv_out_kernel, row=row),
        grid=(m // tm, d // tk),
        in_specs=[
            pl.BlockSpec((tm, tk), lambda i, k: (i, k)),
            pl.BlockSpec((tm, tk), lambda i, k: (i, k)),
            prev, nxt,
            pl.BlockSpec((None, 3, tk), lambda i, k: (jl, 0, k)),
            pl.BlockSpec((None, tk, d), lambda i, k: (jl, k, 0)),
            pl.BlockSpec((tm, d), lambda i, k: (i, 0)),
            _mod_spec(layer, 2, d),
        ],
        out_specs=pl.BlockSpec((tm, d), lambda i, k: (i, 0)),
        out_shape=jax.ShapeDtypeStruct((m, d), F32),
        compiler_params=_params("arbitrary", "arbitrary"),
        name="conv_out",
    )(gb, p, p, p, conv_w, w_out, x, mod)


def _ffn_up_kernel(x_ref, sh_ref, sc_ref, wg_ref, wv_ref, gate_ref, val_ref, h_ref, *, row):
    @pl.when(pl.program_id(1) == 0)
    def _():
        _modulate_into(x_ref, sh_ref, sc_ref, h_ref, row)

    h = h_ref[...]
    gate_ref[...] = _dot(h, wg_ref[...].astype(BF16)).astype(BF16)
    val_ref[...] = _dot(h, wv_ref[...].astype(BF16)).astype(BF16)


def _ffn_up(x, mod, layer, w_up, row, tm, tn=512):
    m, d = x.shape
    f = w_up.shape[2] // 2
    nt = f // tn
    out = jax.ShapeDtypeStruct((m, f), BF16)
    return pl.pallas_call(
        functools.partial(_ffn_up_kernel, row=row),
        grid=(m // tm, nt),
        in_specs=[
            pl.BlockSpec((tm, d), lambda i, j: (i, 0)),
            _mod_spec(layer, 3, d),
            _mod_spec(layer, 4, d),
            pl.BlockSpec((None, d, tn), lambda i, j: (layer, 0, j)),
            pl.BlockSpec((None, d, tn), lambda i, j: (layer, 0, nt + j)),
        ],
        out_specs=[pl.BlockSpec((tm, tn), lambda i, j: (i, j))] * 2,
        out_shape=[out, out],
        scratch_shapes=[pltpu.VMEM((tm, d), BF16)],
        compiler_params=_params("arbitrary", "arbitrary"),
        name="ffn_up",
    )(x, mod, mod, w_up, w_up)


def _ffn_down_kernel(gate_ref, gp_ref, gn_ref, val_ref, cw_ref, cb_ref, w_ref, x_ref, g_ref, o_ref, *, row):
    prev_row, next_row = _halo_rows(gp_ref, gn_ref)
    gate = _conv3(gate_ref[...].astype(F32), prev_row, next_row, cw_ref) + cb_ref[...]
    a = jax.nn.silu(gate) * val_ref[...].astype(F32)
    _accumulate_residual(a.astype(BF16), w_ref, x_ref, g_ref, o_ref, row)


def _ffn_down(gate, val, conv_w, conv_b, w_down, x, mod, layer, row, tm, tk=512):
    m, d = x.shape
    f = gate.shape[1]
    prev, nxt = _halo_specs(tm, tk, m)
    return pl.pallas_call(
        functools.partial(_ffn_down_kernel, row=row),
        grid=(m // tm, f // tk),
        in_specs=[
            pl.BlockSpec((tm, tk), lambda i, k: (i, k)),
            prev, nxt,
            pl.BlockSpec((tm, tk), lambda i, k: (i, k)),
            pl.BlockSpec((None, 3, tk), lambda i, k: (layer, 0, k)),
            pl.BlockSpec((None, 1, tk), lambda i, k: (layer, 0, k)),
            pl.BlockSpec((None, tk, d), lambda i, k: (layer, k, 0)),
            pl.BlockSpec((tm, d), lambda i, k: (i, 0)),
            _mod_spec(layer, 5, d),
        ],
        out_specs=pl.BlockSpec((tm, d), lambda i, k: (i, 0)),
        out_shape=jax.ShapeDtypeStruct((m, d), F32),
        compiler_params=_params("arbitrary", "arbitrary"),
        name="ffn_down",
    )(gate, gate, gate, val, conv_w, conv_b.reshape(conv_b.shape[0], 1, f), w_down, x, mod)


def _rope_tables(n):
    rows = n // GRID_W
    row = jnp.repeat(jnp.arange(rows), GRID_W).astype(F32)
    col = jnp.tile(jnp.arange(GRID_W), rows).astype(F32)
    inv = ROPE_BASE ** (-jnp.arange(ROPE_PAIRS, dtype=F32) / ROPE_PAIRS)
    ang = jnp.stack([row[:, None] * inv, col[:, None] * inv], axis=1)
    ang = jnp.broadcast_to(ang[:, :, None, :], (n, 2, 2, ROPE_PAIRS)).reshape(n, HEAD_DIM)
    sign = jnp.where((jnp.arange(HEAD_DIM) & ROPE_PAIRS) == 0, -1.0, 1.0).astype(F32)
    return jnp.cos(ang), jnp.sin(ang) * sign


def kernel(x, c, ctx, c_ctx, w_ada, b_ada, attn_w_qkv, attn_w_o, attn_q_gain, attn_k_gain, attn_sink,
           sc_w_in, sc_conv, sc_w_out, ffn_w_up, ffn_conv, ffn_conv_b, ffn_w_down):
    batch, n, d = x.shape
    l_ctx = ctx.shape[1]
    depth = w_ada.shape[0]
    assert batch == 1 and d == N_KV_HEADS * GROUP * HEAD_DIM
    tm_x = min(1024, n)
    tm_c = l_ctx

    cc = jnp.concatenate([c, c_ctx[None, :], jnp.zeros((MOD_ROWS - 2, d), F32)], axis=0)
    mod = _ada_table(cc, w_ada, b_ada)
    rope_tabs = _rope_tables(n)
    xs = x[0]
    cs = ctx[0]

    for l in range(depth):
        is_attn = (l % N_MIXERS) == 0
        j = l // N_MIXERS
        need_ctx = l < depth - 1
        if is_attn:
            gains = jnp.stack([attn_q_gain[j], attn_k_gain[j]])
            qkv_c = _qkv_proj(cs, mod, l, attn_w_qkv, j, gains, None, ROW_CTX, tm_c)
            qkv = _qkv_proj(xs, mod, l, attn_w_qkv, j, gains, rope_tabs, ROW_LATENT, tm_x)
            o = _band_attention(qkv, qkv_c, attn_sink[j])
            xs = _attn_out(o, attn_w_o, j, xs, mod, l, ROW_LATENT, tm_x)
            if need_ctx:
                o_c = _ctx_attention(qkv_c, attn_sink[j])
                cs = _attn_out(o_c, attn_w_o, j, cs, mod, l, ROW_CTX, tm_c)
        else:
            gb, p = _conv_in(xs, mod, l, sc_w_in, j, ROW_LATENT, tm_x)
            xs = _conv_out(gb, p, sc_conv, sc_w_out, j, xs, mod, l, ROW_LATENT, tm_x)
            if need_ctx:
                gb_c, p_c = _conv_in(cs, mod, l, sc_w_in, j, ROW_CTX, tm_c)
                cs = _conv_out(gb_c, p_c, sc_conv, sc_w_out, j, cs, mod, l, ROW_CTX, tm_c)
        gate, val = _ffn_up(xs, mod, l, ffn_w_up, ROW_LATENT, tm_x)
        xs = _ffn_down(gate, val, ffn_conv, ffn_conv_b, ffn_w_down, xs, mod, l, ROW_LATENT, tm_x)
        if need_ctx:
            gate_c, val_c = _ffn_up(cs, mod, l, ffn_w_up, ROW_CTX, tm_c)
            cs = _ffn_down(gate_c, val_c, ffn_conv, ffn_conv_b, ffn_w_down, cs, mod, l, ROW_CTX, tm_c)
    return xs[None]
```

```python
import functools

import jax
import jax.numpy as jnp
from jax import lax
from jax.experimental import pallas as pl
from jax.experimental.pallas import tpu as pltpu

HEAD_DIM = 128
N_KV_HEADS = 4
GROUP = 4
WINDOW = 128
GRID_W = 64
ROPE_PAIRS = HEAD_DIM // 4
ROPE_BASE = 10000.0
N_MIXERS = 2
EPS = 1e-6
NEG_INF = -1e30

BF16 = jnp.bfloat16
F32 = jnp.float32

V7X_VMEM_LIMIT_BYTES = 56 * 1024 * 1024
HALO_ROWS = 16
MOD_ROWS = 8
ROW_LATENT = 0
ROW_CTX = 1
ROW_CHUNK = 512
OUT_ROW_CHUNK = 256
LOG2E = 1.4426950408889634
SHIFT_MIX, SCALE_MIX, GATE_MIX, SHIFT_FFN, SCALE_FFN, GATE_FFN = range(6)


def _params(*sem):
    return pltpu.CompilerParams(dimension_semantics=sem, vmem_limit_bytes=V7X_VMEM_LIMIT_BYTES)


def _dot(a, b):
    return jnp.dot(a, b, preferred_element_type=F32)


def _ada_kernel(cc_ref, w_ref, b_ref, o_ref):
    a = jax.nn.silu(cc_ref[...]).astype(BF16)
    o_ref[...] = _dot(a, w_ref[...].astype(BF16)) + b_ref[...]


def _ada_table(cc, w_ada, b_ada, tn=1024):
    depth, d, d6 = w_ada.shape
    return pl.pallas_call(
        _ada_kernel,
        grid=(depth, d6 // tn),
        in_specs=[
            pl.BlockSpec((MOD_ROWS, d), lambda l, j: (0, 0)),
            pl.BlockSpec((None, d, tn), lambda l, j: (l, 0, j)),
            pl.BlockSpec((None, 1, tn), lambda l, j: (l, 0, j)),
        ],
        out_specs=pl.BlockSpec((None, MOD_ROWS, tn), lambda l, j: (l, 0, j)),
        out_shape=jax.ShapeDtypeStruct((depth, MOD_ROWS, d6), F32),
        compiler_params=_params("arbitrary", "arbitrary"),
        name="ada_table",
    )(cc, w_ada, b_ada.reshape(depth, 1, d6))


def _modulated(x, sh_ref, sc_ref, row):
    r = lax.rsqrt(jnp.mean(x * x, axis=-1, keepdims=True) + EPS)
    return ((x * r) * (1 + sc_ref[row:row + 1, :]) + sh_ref[row:row + 1, :]).astype(BF16)


def _mod_spec(layer, chunk, d):
    return pl.BlockSpec((None, MOD_ROWS, d), lambda *_: (layer, 0, chunk))


def _modulate_kernel(x_ref, sh_ref, sc_ref, h_ref, *, row):
    h_ref[...] = _modulated(x_ref[...], sh_ref, sc_ref, row)


def _modulate(x, mod, layer, row, tm):
    m, d = x.shape
    return pl.pallas_call(
        functools.partial(_modulate_kernel, row=row),
        grid=(m // tm,),
        in_specs=[pl.BlockSpec((tm, d), lambda i: (i, 0)), _mod_spec(layer, SHIFT_MIX, d), _mod_spec(layer, SCALE_MIX, d)],
        out_specs=pl.BlockSpec((tm, d), lambda i: (i, 0)),
        out_shape=jax.ShapeDtypeStruct((m, d), BF16),
        compiler_params=_params("arbitrary"),
        name="modulate",
    )(x, mod, mod)


def _fill_halo_lhs(h_ref, hp_ref, hn_ref, hs_ref):
    i = pl.program_id(0)
    tm = h_ref.shape[0]
    zero = jnp.zeros(hp_ref.shape, BF16)
    hs_ref[0:HALO_ROWS, :] = jnp.where(i > 0, hp_ref[...], zero)
    hs_ref[HALO_ROWS:HALO_ROWS + tm, :] = h_ref[...]
    hs_ref[HALO_ROWS + tm:, :] = jnp.where(i < pl.num_programs(0) - 1, hn_ref[...], zero)


def _halo_lhs_specs(tm, d, m):
    per = tm // HALO_ROWS
    nblk = m // HALO_ROWS
    return [
        pl.BlockSpec((tm, d), lambda i, j: (i, 0)),
        pl.BlockSpec((HALO_ROWS, d), lambda i, j: (jnp.maximum(i * per - 1, 0), 0)),
        pl.BlockSpec((HALO_ROWS, d), lambda i, j: (jnp.minimum((i + 1) * per, nblk - 1), 0)),
    ]


def _conv3_rows(y, tm, w_ref):
    ext = y.shape[0]
    sl = slice(HALO_ROWS, HALO_ROWS + tm)
    dn = pltpu.roll(y, 1, 0)[sl]
    up = pltpu.roll(y, ext - 1, 0)[sl]
    return dn * w_ref[0:1, :] + y[sl] * w_ref[1:2, :] + up * w_ref[2:3, :]


def _qkv_kernel(*refs, rope, heads_per_tile, n_q_tiles, n_qk_tiles):
    if rope:
        h_ref, w_ref, gain_ref, cos_ref, sin_ref, o_ref = refs
    else:
        h_ref, w_ref, gain_ref, o_ref = refs
    j = pl.program_id(1)
    tm = h_ref.shape[0]
    rc = min(ROW_CHUNK, tm)
    w = w_ref[...].astype(BF16)

    @pl.when(j < n_qk_tiles)
    def _():
        gain = jnp.where(j < n_q_tiles, gain_ref[0:1, :], gain_ref[1:2, :])
        lane = lax.broadcasted_iota(jnp.int32, (rc, HEAD_DIM), 1)
        first_half = (lane & ROPE_PAIRS) == 0
        for c0 in range(0, tm, rc):
            rows = slice(c0, c0 + rc)
            y = _dot(h_ref[rows, :], w)
            for hh in range(heads_per_tile):
                sl = slice(hh * HEAD_DIM, (hh + 1) * HEAD_DIM)
                yh = y[:, sl]
                r = lax.rsqrt(jnp.mean(yh * yh, axis=-1, keepdims=True) + EPS)
                yh = (yh * r) * gain
                if rope:
                    rot = jnp.where(first_half, pltpu.roll(yh, HEAD_DIM - ROPE_PAIRS, 1), pltpu.roll(yh, ROPE_PAIRS, 1))
                    yh = yh * cos_ref[rows, :] + rot * sin_ref[rows, :]
                o_ref[rows, sl] = yh.astype(BF16)

    @pl.when(j >= n_qk_tiles)
    def _():
        for c0 in range(0, tm, rc):
            rows = slice(c0, c0 + rc)
            o_ref[rows, :] = _dot(h_ref[rows, :], w).astype(BF16)


def _qkv_proj(h, w_qkv, jl, gains, rope_tabs, tm, tn=512):
    m, d = h.shape
    n_out = w_qkv.shape[2]
    d_q = d
    d_kv = (n_out - d_q) // 2
    rope = rope_tabs is not None
    in_specs = [
        pl.BlockSpec((tm, d), lambda i, j: (i, 0)),
        pl.BlockSpec((None, d, tn), lambda i, j: (jl, 0, j)),
        pl.BlockSpec((2, HEAD_DIM), lambda i, j: (0, 0)),
    ]
    args = [h, w_qkv, gains]
    if rope:
        in_specs += [pl.BlockSpec((tm, HEAD_DIM), lambda i, j: (i, 0))] * 2
        args += list(rope_tabs)
    kern = functools.partial(_qkv_kernel, rope=rope, heads_per_tile=tn // HEAD_DIM,
                             n_q_tiles=d_q // tn, n_qk_tiles=(d_q + d_kv) // tn)
    return pl.pallas_call(
        kern,
        grid=(m // tm, n_out // tn),
        in_specs=in_specs,
        out_specs=pl.BlockSpec((tm, tn), lambda i, j: (i, j)),
        out_shape=jax.ShapeDtypeStruct((m, n_out), BF16),
        compiler_params=_params("arbitrary", "arbitrary"),
        name="qkv_proj",
    )(*args)


def _sink_softmax_pv(t, v, sink2):
    m2 = jnp.maximum(jnp.max(t, axis=-1, keepdims=True), sink2)
    e = jnp.exp2(t - m2)
    den = jnp.sum(e, axis=-1, keepdims=True) + jnp.exp2(sink2 - m2)
    return _dot(e.astype(BF16), v) / den


def _scores2(q, k):
    return lax.dot_general(q, k, (((1,), (1,)), ((), ())), preferred_element_type=F32) * (HEAD_DIM ** -0.5 * LOG2E)


def _band_attn_kernel(sink_ref, q_ref, kp_ref, km_ref, kn_ref, kc_ref, vp_ref, vm_ref, vn_ref, vc_ref, o_ref,
                      kw_ref, vw_ref):
    h = pl.program_id(0)
    i = pl.program_id(1)
    tq = q_ref.shape[0]
    nb = tq // WINDOW
    last_blk = pl.num_programs(1) * nb - 1
    kw_ref[0:WINDOW, :] = kp_ref[...]
    kw_ref[WINDOW:WINDOW + tq, :] = km_ref[...]
    kw_ref[WINDOW + tq:, :] = kn_ref[...]
    vw_ref[0:WINDOW, :] = vp_ref[...]
    vw_ref[WINDOW:WINDOW + tq, :] = vm_ref[...]
    vw_ref[WINDOW + tq:, :] = vn_ref[...]
    rows = GROUP * WINDOW
    r = lax.broadcasted_iota(jnp.int32, (rows, WINDOW), 0) & (WINDOW - 1)
    c_minus_r = lax.broadcasted_iota(jnp.int32, (rows, WINDOW), 1) - r
    sink2 = jnp.concatenate([jnp.full((WINDOW, 1), sink_ref[h * GROUP + g] * LOG2E, F32) for g in range(GROUP)], axis=0)
    kc = kc_ref[...]
    vc = vc_ref[...]
    for b in range(nb):
        blk = i * nb + b
        qs = slice(b * WINDOW, (b + 1) * WINDOW)
        q4 = jnp.concatenate([q_ref[qs, g * HEAD_DIM:(g + 1) * HEAD_DIM] for g in range(GROUP)], axis=0)
        ks = slice(b * WINDOW, (b + 3) * WINDOW)
        s = _scores2(q4, jnp.concatenate([kw_ref[ks, :], kc], axis=0))
        lo = jnp.where(blk > 0, 0, WINDOW)
        hi = jnp.where(blk < last_blk, 0, -WINDOW)
        s_prev = jnp.where(c_minus_r >= lo, s[:, 0:WINDOW], NEG_INF)
        s_next = jnp.where(c_minus_r <= hi, s[:, 2 * WINDOW:3 * WINDOW], NEG_INF)
        s = jnp.concatenate([s_prev, s[:, WINDOW:2 * WINDOW], s_next, s[:, 3 * WINDOW:]], axis=1)
        o = _sink_softmax_pv(s, jnp.concatenate([vw_ref[ks, :], vc], axis=0), sink2)
        for g in range(GROUP):
            o_ref[qs, g * HEAD_DIM:(g + 1) * HEAD_DIM] = o[g * WINDOW:(g + 1) * WINDOW, :].astype(BF16)


def _band_attention(qkv, qkv_c, sink, tq):
    n = qkv.shape[0]
    l_ctx = qkv_c.shape[0]
    d_q = N_KV_HEADS * GROUP * HEAD_DIM
    kcol = d_q // HEAD_DIM
    vcol = kcol + N_KV_HEADS
    per = tq // WINDOW
    nblk = n // WINDOW

    def prev(col0):
        return pl.BlockSpec((WINDOW, HEAD_DIM), lambda h, i: (jnp.maximum(i * per - 1, 0), col0 + h))

    def main(col0):
        return pl.BlockSpec((tq, HEAD_DIM), lambda h, i: (i, col0 + h))

    def nxt(col0):
        return pl.BlockSpec((WINDOW, HEAD_DIM), lambda h, i: (jnp.minimum((i + 1) * per, nblk - 1), col0 + h))

    def ctx(col0):
        return pl.BlockSpec((l_ctx, HEAD_DIM), lambda h, i: (0, col0 + h))

    return pl.pallas_call(
        _band_attn_kernel,
        grid=(N_KV_HEADS, n // tq),
        in_specs=[
            pl.BlockSpec(memory_space=pltpu.SMEM),
            pl.BlockSpec((tq, GROUP * HEAD_DIM), lambda h, i: (i, h)),
            prev(kcol), main(kcol), nxt(kcol), ctx(kcol),
            prev(vcol), main(vcol), nxt(vcol), ctx(vcol),
        ],
        out_specs=pl.BlockSpec((tq, GROUP * HEAD_DIM), lambda h, i: (i, h)),
        out_shape=jax.ShapeDtypeStruct((n, d_q), BF16),
        scratch_shapes=[pltpu.VMEM((tq + 2 * WINDOW, HEAD_DIM), BF16)] * 2,
        compiler_params=_params("arbitrary", "arbitrary"),
        name="band_attention",
    )(sink, qkv, qkv, qkv, qkv, qkv_c, qkv, qkv, qkv, qkv_c)


def _ctx_attn_kernel(sink_ref, q_ref, k_ref, v_ref, o_ref):
    h = pl.program_id(0)
    for g in range(GROUP):
        sl = slice(g * HEAD_DIM, (g + 1) * HEAD_DIM)
        o = _sink_softmax_pv(_scores2(q_ref[:, sl], k_ref[...]), v_ref[...], sink_ref[h * GROUP + g] * LOG2E)
        o_ref[:, sl] = o.astype(BF16)


def _ctx_attention(qkv_c, sink):
    l_ctx = qkv_c.shape[0]
    d_q = N_KV_HEADS * GROUP * HEAD_DIM
    kcol = d_q // HEAD_DIM
    vcol = kcol + N_KV_HEADS
    return pl.pallas_call(
        _ctx_attn_kernel,
        grid=(N_KV_HEADS,),
        in_specs=[
            pl.BlockSpec(memory_space=pltpu.SMEM),
            pl.BlockSpec((l_ctx, GROUP * HEAD_DIM), lambda h: (0, h)),
            pl.BlockSpec((l_ctx, HEAD_DIM), lambda h: (0, kcol + h)),
            pl.BlockSpec((l_ctx, HEAD_DIM), lambda h: (0, vcol + h)),
        ],
        out_specs=pl.BlockSpec((l_ctx, GROUP * HEAD_DIM), lambda h: (0, h)),
        out_shape=jax.ShapeDtypeStruct((l_ctx, d_q), BF16),
        compiler_params=_params("arbitrary"),
        name="ctx_attention",
    )(sink, qkv_c, qkv_c, qkv_c)


def _conv_in_kernel(h_ref, hp_ref, hn_ref, wb_ref, wc_ref, wv_ref, cw_ref, u_ref, hs_ref):
    @pl.when(pl.program_id(1) == 0)
    def _():
        _fill_halo_lhs(h_ref, hp_ref, hn_ref, hs_ref)

    tm = h_ref.shape[0]
    rc = min(ROW_CHUNK, tm)
    wb = wb_ref[...].astype(BF16)
    wc = wc_ref[...].astype(BF16)
    wv = wv_ref[...].astype(BF16)
    for c0 in range(0, tm, rc):
        hs = hs_ref[c0:c0 + rc + 2 * HALO_ROWS, :]
        p = _dot(hs, wc) * _dot(hs, wv)
        gate_b = _dot(h_ref[c0:c0 + rc, :], wb)
        u_ref[c0:c0 + rc, :] = (gate_b * _conv3_rows(p, rc, cw_ref)).astype(BF16)


def _conv_in(h, w_in, jl, conv_w, tm, tn=256):
    m, d = h.shape
    nt = d // tn

    def wspec(part):
        return pl.BlockSpec((None, d, tn), lambda i, j: (jl, 0, part * nt + j))

    return pl.pallas_call(
        _conv_in_kernel,
        grid=(m // tm, nt),
        in_specs=_halo_lhs_specs(tm, d, m) + [wspec(0), wspec(1), wspec(2),
                                              pl.BlockSpec((None, 3, tn), lambda i, j: (jl, 0, j))],
        out_specs=pl.BlockSpec((tm, tn), lambda i, j: (i, j)),
        out_shape=jax.ShapeDtypeStruct((m, d), BF16),
        scratch_shapes=[pltpu.VMEM((tm + 2 * HALO_ROWS, d), BF16)],
        compiler_params=_params("arbitrary", "arbitrary"),
        name="conv_in",
    )(h, h, h, w_in, w_in, w_in, conv_w)


def _ffn_up_kernel(h_ref, hp_ref, hn_ref, wg_ref, wv_ref, cw_ref, cb_ref, a_ref, hs_ref):
    @pl.when(pl.program_id(1) == 0)
    def _():
        _fill_halo_lhs(h_ref, hp_ref, hn_ref, hs_ref)

    tm = h_ref.shape[0]
    rc = min(ROW_CHUNK, tm)
    wg = wg_ref[...].astype(BF16)
    wv = wv_ref[...].astype(BF16)
    for c0 in range(0, tm, rc):
        gate = _conv3_rows(_dot(hs_ref[c0:c0 + rc + 2 * HALO_ROWS, :], wg), rc, cw_ref) + cb_ref[...]
        val = _dot(h_ref[c0:c0 + rc, :], wv)
        a_ref[c0:c0 + rc, :] = (jax.nn.silu(gate) * val).astype(BF16)


def _ffn_up(h, w_up, layer, conv_w, conv_b, tm, tn=256):
    m, d = h.shape
    f = w_up.shape[2] // 2
    nt = f // tn
    return pl.pallas_call(
        _ffn_up_kernel,
        grid=(m // tm, nt),
        in_specs=_halo_lhs_specs(tm, d, m) + [
            pl.BlockSpec((None, d, tn), lambda i, j: (layer, 0, j)),
            pl.BlockSpec((None, d, tn), lambda i, j: (layer, 0, nt + j)),
            pl.BlockSpec((None, 3, tn), lambda i, j: (layer, 0, j)),
            pl.BlockSpec((None, 1, tn), lambda i, j: (layer, 0, j)),
        ],
        out_specs=pl.BlockSpec((tm, tn), lambda i, j: (i, j)),
        out_shape=jax.ShapeDtypeStruct((m, f), BF16),
        scratch_shapes=[pltpu.VMEM((tm + 2 * HALO_ROWS, d), BF16)],
        compiler_params=_params("arbitrary", "arbitrary"),
        name="ffn_up",
    )(h, h, h, w_up, w_up, conv_w, conv_b.reshape(conv_b.shape[0], 1, f))


def _out_proj_kernel(*refs, row, has_next, n_k):
    if has_next:
        a_ref, w_ref, x_ref, g_ref, sh_ref, sc_ref, o_ref, hn_ref = refs
    else:
        a_ref, w_ref, x_ref, g_ref, o_ref = refs
    k = pl.program_id(1)
    last = n_k - 1
    tm = a_ref.shape[0]
    rc = min(OUT_ROW_CHUNK, tm)
    chunks = [slice(c0, c0 + rc) for c0 in range(0, tm, rc)]

    def finish(rows, acc):
        xn = x_ref[rows, :] + g_ref[row:row + 1, :] * acc
        o_ref[rows, :] = xn
        if has_next:
            hn_ref[rows, :] = _modulated(xn, sh_ref, sc_ref, row)

    if n_k == 1:
        for rows in chunks:
            finish(rows, _dot(a_ref[rows, :], w_ref[...]))
        return

    @pl.when(k == 0)
    def _():
        for rows in chunks:
            o_ref[rows, :] = _dot(a_ref[rows, :], w_ref[...])

    @pl.when((k > 0) & (k < last))
    def _():
        for rows in chunks:
            o_ref[rows, :] += _dot(a_ref[rows, :], w_ref[...])

    @pl.when(k == last)
    def _():
        for rows in chunks:
            finish(rows, o_ref[rows, :] + _dot(a_ref[rows, :], w_ref[...]))


def _out_proj(a, w_bf16, jl, x, mod, layer, gate_chunk, nxt, row, tm, tk=512):
    m, d = x.shape
    kdim = a.shape[1]
    in_specs = [
        pl.BlockSpec((tm, tk), lambda i, k: (i, k)),
        pl.BlockSpec((None, tk, d), lambda i, k: (jl, k, 0)),
        pl.BlockSpec((tm, d), lambda i, k: (i, 0)),
        _mod_spec(layer, gate_chunk, d),
    ]
    args = [a, w_bf16, x, mod]
    out_specs = [pl.BlockSpec((tm, d), lambda i, k: (i, 0))]
    out_shape = [jax.ShapeDtypeStruct((m, d), F32)]
    if nxt is not None:
        in_specs += [_mod_spec(nxt[0], nxt[1], d), _mod_spec(nxt[0], nxt[2], d)]
        args += [mod, mod]
        out_specs.append(pl.BlockSpec((tm, d), lambda i, k: (i, 0)))
        out_shape.append(jax.ShapeDtypeStruct((m, d), BF16))
    outs = pl.pallas_call(
        functools.partial(_out_proj_kernel, row=row, has_next=nxt is not None, n_k=kdim // tk),
        grid=(m // tm, kdim // tk),
        in_specs=in_specs,
        out_specs=out_specs,
        out_shape=out_shape,
        compiler_params=_params("arbitrary", "arbitrary"),
        name="out_proj",
    )(*args)
    return (outs[0], outs[1]) if nxt is not None else (outs[0], None)


def _rope_tables(n):
    rows = n // GRID_W
    row = jnp.repeat(jnp.arange(rows), GRID_W).astype(F32)
    col = jnp.tile(jnp.arange(GRID_W), rows).astype(F32)
    inv = ROPE_BASE ** (-jnp.arange(ROPE_PAIRS, dtype=F32) / ROPE_PAIRS)
    ang = jnp.stack([row[:, None] * inv, col[:, None] * inv], axis=1)
    ang = jnp.broadcast_to(ang[:, :, None, :], (n, 2, 2, ROPE_PAIRS)).reshape(n, HEAD_DIM)
    sign = jnp.where((jnp.arange(HEAD_DIM) & ROPE_PAIRS) == 0, -1.0, 1.0).astype(F32)
    return jnp.cos(ang), jnp.sin(ang) * sign


def kernel(x, c, ctx, c_ctx, w_ada, b_ada, attn_w_qkv, attn_w_o, attn_q_gain, attn_k_gain, attn_sink,
           sc_w_in, sc_conv, sc_w_out, ffn_w_up, ffn_conv, ffn_conv_b, ffn_w_down):
    batch, n, d = x.shape
    l_ctx = ctx.shape[1]
    depth = w_ada.shape[0]
    assert batch == 1 and d == N_KV_HEADS * GROUP * HEAD_DIM
    tm_in = min(2048, n)
    tm_out = min(1024, n)
    tq = min(512, n)
    tm_c = l_ctx

    cc = jnp.concatenate([c, c_ctx[None, :], jnp.zeros((MOD_ROWS - 2, d), F32)], axis=0)
    mod = _ada_table(cc, w_ada, b_ada)
    rope_tabs = _rope_tables(n)
    w_o = attn_w_o.astype(BF16)
    w_out = sc_w_out.astype(BF16)
    w_down = ffn_w_down.astype(BF16)
    xs, cs = x[0], ctx[0]
    hx = _modulate(xs, mod, 0, ROW_LATENT, min(512, n))
    hc = _modulate(cs, mod, 0, ROW_CTX, tm_c)

    for l in range(depth):
        is_attn = (l % N_MIXERS) == 0
        j = l // N_MIXERS
        need_ctx = l < depth - 1
        nxt_ffn = (l, SHIFT_FFN, SCALE_FFN)
        nxt_mix = (l + 1, SHIFT_MIX, SCALE_MIX) if l + 1 < depth else None
        if is_attn:
            gains = jnp.stack([attn_q_gain[j], attn_k_gain[j]])
            qkv_c = _qkv_proj(hc, attn_w_qkv, j, gains, None, tm_c)
            qkv = _qkv_proj(hx, attn_w_qkv, j, gains, rope_tabs, tm_in)
            o = _band_attention(qkv, qkv_c, attn_sink[j], tq)
            xs, hx = _out_proj(o, w_o, j, xs, mod, l, GATE_MIX, nxt_ffn, ROW_LATENT, tm_out)
            if need_ctx:
                o_c = _ctx_attention(qkv_c, attn_sink[j])
                cs, hc = _out_proj(o_c, w_o, j, cs, mod, l, GATE_MIX, nxt_ffn, ROW_CTX, tm_c)
        else:
            u = _conv_in(hx, sc_w_in, j, sc_conv, tm_in)
            xs, hx = _out_proj(u, w_out, j, xs, mod, l, GATE_MIX, nxt_ffn, ROW_LATENT, tm_out)
            if need_ctx:
                u_c = _conv_in(hc, sc_w_in, j, sc_conv, tm_c)
                cs, hc = _out_proj(u_c, w_out, j, cs, mod, l, GATE_MIX, nxt_ffn, ROW_CTX, tm_c)
        a = _ffn_up(hx, ffn_w_up, l, ffn_conv, ffn_conv_b, tm_in)
        xs, hx = _out_proj(a, w_down, l, xs, mod, l, GATE_FFN, nxt_mix, ROW_LATENT, tm_out)
        if need_ctx:
            a_c = _ffn_up(hc, ffn_w_up, l, ffn_conv, ffn_conv_b, tm_c)
            cs, hc = _out_proj(a_c, w_down, l, cs, mod, l, GATE_FFN, nxt_mix, ROW_CTX, tm_c)
    return xs[None]
```

```python
import functools

import jax
import jax.numpy as jnp
from jax import lax
from jax.experimental import pallas as pl
from jax.experimental.pallas import tpu as pltpu

HEAD_DIM = 128
N_KV_HEADS = 4
GROUP = 4
WINDOW = 128
GRID_W = 64
ROPE_PAIRS = HEAD_DIM // 4
ROPE_BASE = 10000.0
N_MIXERS = 2
EPS = 1e-6
NEG_INF = -1e30

BF16 = jnp.bfloat16
F32 = jnp.float32

V7X_VMEM_LIMIT_BYTES = 56 * 1024 * 1024
HALO_ROWS = 16
MOD_ROWS = 8
ROW_LATENT = 0
ROW_CTX = 1
ROW_CHUNK = 512
OUT_ROW_CHUNK = 256
LOG2E = 1.4426950408889634
SCORE_SCALE2 = HEAD_DIM ** -0.5 * LOG2E
SHIFT_MIX, SCALE_MIX, GATE_MIX, SHIFT_FFN, SCALE_FFN, GATE_FFN = range(6)


def _params(*sem):
    return pltpu.CompilerParams(dimension_semantics=sem, vmem_limit_bytes=V7X_VMEM_LIMIT_BYTES)


def _dot(a, b):
    return jnp.dot(a, b, preferred_element_type=F32)


def _ada_kernel(cc_ref, w_ref, b_ref, o_ref):
    a = jax.nn.silu(cc_ref[...]).astype(BF16)
    o_ref[...] = _dot(a, w_ref[...].astype(BF16)) + b_ref[...]


def _ada_table(cc, w_ada, b_ada, tn=1024):
    depth, d, d6 = w_ada.shape
    return pl.pallas_call(
        _ada_kernel,
        grid=(depth, d6 // tn),
        in_specs=[
            pl.BlockSpec((MOD_ROWS, d), lambda l, j: (0, 0)),
            pl.BlockSpec((None, d, tn), lambda l, j: (l, 0, j)),
            pl.BlockSpec((None, 1, tn), lambda l, j: (l, 0, j)),
        ],
        out_specs=pl.BlockSpec((None, MOD_ROWS, tn), lambda l, j: (l, 0, j)),
        out_shape=jax.ShapeDtypeStruct((depth, MOD_ROWS, d6), F32),
        compiler_params=_params("arbitrary", "arbitrary"),
        name="ada_table",
    )(cc, w_ada, b_ada.reshape(depth, 1, d6))


def _modulated(x, sh_ref, sc_ref, row):
    r = lax.rsqrt(jnp.mean(x * x, axis=-1, keepdims=True) + EPS)
    return ((x * r) * (1 + sc_ref[row:row + 1, :]) + sh_ref[row:row + 1, :]).astype(BF16)


def _mod_spec(layer, chunk, d):
    return pl.BlockSpec((None, MOD_ROWS, d), lambda *_: (layer, 0, chunk))


def _modulate_kernel(x_ref, sh_ref, sc_ref, h_ref, *, row):
    h_ref[...] = _modulated(x_ref[...], sh_ref, sc_ref, row)


def _modulate(x, mod, layer, row, tm):
    m, d = x.shape
    return pl.pallas_call(
        functools.partial(_modulate_kernel, row=row),
        grid=(m // tm,),
        in_specs=[pl.BlockSpec((tm, d), lambda i: (i, 0)), _mod_spec(layer, SHIFT_MIX, d), _mod_spec(layer, SCALE_MIX, d)],
        out_specs=pl.BlockSpec((tm, d), lambda i: (i, 0)),
        out_shape=jax.ShapeDtypeStruct((m, d), BF16),
        compiler_params=_params("arbitrary"),
        name="modulate",
    )(x, mod, mod)


def _fill_halo_lhs(h_ref, hp_ref, hn_ref, hs_ref):
    i = pl.program_id(0)
    tm = h_ref.shape[0]
    zero = jnp.zeros(hp_ref.shape, BF16)
    hs_ref[0:HALO_ROWS, :] = jnp.where(i > 0, hp_ref[...], zero)
    hs_ref[HALO_ROWS:HALO_ROWS + tm, :] = h_ref[...]
    hs_ref[HALO_ROWS + tm:, :] = jnp.where(i < pl.num_programs(0) - 1, hn_ref[...], zero)


def _halo_lhs_specs(tm, d, m):
    per = tm // HALO_ROWS
    nblk = m // HALO_ROWS
    return [
        pl.BlockSpec((tm, d), lambda i, j: (i, 0)),
        pl.BlockSpec((HALO_ROWS, d), lambda i, j: (jnp.maximum(i * per - 1, 0), 0)),
        pl.BlockSpec((HALO_ROWS, d), lambda i, j: (jnp.minimum((i + 1) * per, nblk - 1), 0)),
    ]


def _conv3_rows(y, tm, w_ref):
    ext = y.shape[0]
    sl = slice(HALO_ROWS, HALO_ROWS + tm)
    dn = pltpu.roll(y, 1, 0)[sl]
    up = pltpu.roll(y, ext - 1, 0)[sl]
    return dn * w_ref[0:1, :] + y[sl] * w_ref[1:2, :] + up * w_ref[2:3, :]


def _paired_lanes(a, n_heads):
    lead = a.shape[:-1]
    return a.reshape(*lead, n_heads, 2, 2, ROPE_PAIRS).swapaxes(-3, -2).reshape(*lead, n_heads * HEAD_DIM)


def _qkv_kernel(*refs, rope, tn, n_q_tiles, n_qk_tiles):
    if rope:
        h_ref, w_ref, gain_ref, cos_ref, sin_ref, o_ref = refs
    else:
        h_ref, w_ref, gain_ref, o_ref = refs
    h = h_ref[...]
    for jt in range(w_ref.shape[1] // tn):
        y = _dot(h, w_ref[:, jt * tn:(jt + 1) * tn])
        if jt >= n_qk_tiles:
            o_ref[:, jt * tn:(jt + 1) * tn] = y.astype(BF16)
            continue
        gain = gain_ref[0:1, :] * SCORE_SCALE2 if jt < n_q_tiles else gain_ref[1:2, :]
        for hh in range(tn // HEAD_DIM):
            yh = y[:, hh * HEAD_DIM:(hh + 1) * HEAD_DIM]
            r = lax.rsqrt(jnp.mean(yh * yh, axis=-1, keepdims=True) + EPS)
            yh = (yh * r) * gain
            if rope:
                yh = yh * cos_ref[...] + pltpu.roll(yh, HEAD_DIM // 2, 1) * sin_ref[...]
            o_ref[:, jt * tn + hh * HEAD_DIM:jt * tn + (hh + 1) * HEAD_DIM] = yh.astype(BF16)


def _qkv_proj(h, w_qkv_paired, jl, gains_paired, rope_tabs, tm, tn=512):
    m, d = h.shape
    n_out = w_qkv_paired.shape[2]
    d_q = d
    d_kv = (n_out - d_q) // 2
    rope = rope_tabs is not None
    in_specs = [
        pl.BlockSpec((tm, d), lambda i: (i, 0)),
        pl.BlockSpec((None, d, n_out), lambda i: (jl, 0, 0), pipeline_mode=pl.Buffered(1)),
        pl.BlockSpec((2, HEAD_DIM), lambda i: (0, 0)),
    ]
    args = [h, w_qkv_paired, gains_paired]
    if rope:
        in_specs += [pl.BlockSpec((tm, HEAD_DIM), lambda i: (i, 0))] * 2
        args += list(rope_tabs)
    kern = functools.partial(_qkv_kernel, rope=rope, tn=tn, n_q_tiles=d_q // tn, n_qk_tiles=(d_q + d_kv) // tn)
    return pl.pallas_call(
        kern,
        grid=(m // tm,),
        in_specs=in_specs,
        out_specs=pl.BlockSpec((tm, n_out), lambda i: (i, 0)),
        out_shape=jax.ShapeDtypeStruct((m, n_out), BF16),
        compiler_params=_params("arbitrary"),
        name="qkv_proj",
    )(*args)


def _sink_softmax_pv(t, v, sink2):
    m2 = jnp.maximum(jnp.max(t, axis=-1, keepdims=True), sink2)
    e = jnp.exp2(t - m2)
    den = jnp.sum(e, axis=-1, keepdims=True) + jnp.exp2(sink2 - m2)
    return _dot(e.astype(BF16), v) / den


def _scores2(q, k):
    return lax.dot_general(q, k, (((1,), (1,)), ((), ())), preferred_element_type=F32)


def _band_attn_kernel(sink_ref, q_ref, kp_ref, km_ref, kn_ref, kc_ref, vp_ref, vm_ref, vn_ref, vc_ref, o_ref,
                      kw_ref, vw_ref):
    h = pl.program_id(0)
    i = pl.program_id(1)
    tq = q_ref.shape[0]
    nb = tq // WINDOW
    last_blk = pl.num_programs(1) * nb - 1
    kw_ref[0:WINDOW, :] = kp_ref[...]
    kw_ref[WINDOW:WINDOW + tq, :] = km_ref[...]
    kw_ref[WINDOW + tq:, :] = kn_ref[...]
    vw_ref[0:WINDOW, :] = vp_ref[...]
    vw_ref[WINDOW:WINDOW + tq, :] = vm_ref[...]
    vw_ref[WINDOW + tq:, :] = vn_ref[...]
    rows = GROUP * WINDOW
    r = lax.broadcasted_iota(jnp.int32, (rows, WINDOW), 0) & (WINDOW - 1)
    c_minus_r = lax.broadcasted_iota(jnp.int32, (rows, WINDOW), 1) - r
    sink2 = jnp.concatenate([jnp.full((WINDOW, 1), sink_ref[h * GROUP + g] * LOG2E, F32) for g in range(GROUP)], axis=0)
    kc = kc_ref[...]
    vc = vc_ref[...]
    for b in range(nb):
        blk = i * nb + b
        qs = slice(b * WINDOW, (b + 1) * WINDOW)
        q4 = jnp.concatenate([q_ref[qs, g * HEAD_DIM:(g + 1) * HEAD_DIM] for g in range(GROUP)], axis=0)
        ks = slice(b * WINDOW, (b + 3) * WINDOW)
        s = _scores2(q4, jnp.concatenate([kw_ref[ks, :], kc], axis=0))
        lo = jnp.where(blk > 0, 0, WINDOW)
        hi = jnp.where(blk < last_blk, 0, -WINDOW)
        s_prev = jnp.where(c_minus_r >= lo, s[:, 0:WINDOW], NEG_INF)
        s_next = jnp.where(c_minus_r <= hi, s[:, 2 * WINDOW:3 * WINDOW], NEG_INF)
        s = jnp.concatenate([s_prev, s[:, WINDOW:2 * WINDOW], s_next, s[:, 3 * WINDOW:]], axis=1)
        o = _sink_softmax_pv(s, jnp.concatenate([vw_ref[ks, :], vc], axis=0), sink2)
        for g in range(GROUP):
            o_ref[qs, g * HEAD_DIM:(g + 1) * HEAD_DIM] = o[g * WINDOW:(g + 1) * WINDOW, :].astype(BF16)


def _band_attention(qkv, qkv_c, sink, tq):
    n = qkv.shape[0]
    l_ctx = qkv_c.shape[0]
    d_q = N_KV_HEADS * GROUP * HEAD_DIM
    kcol = d_q // HEAD_DIM
    vcol = kcol + N_KV_HEADS
    per = tq // WINDOW
    nblk = n // WINDOW

    def prev(col0):
        return pl.BlockSpec((WINDOW, HEAD_DIM), lambda h, i: (jnp.maximum(i * per - 1, 0), col0 + h))

    def main(col0):
        return pl.BlockSpec((tq, HEAD_DIM), lambda h, i: (i, col0 + h))

    def nxt(col0):
        return pl.BlockSpec((WINDOW, HEAD_DIM), lambda h, i: (jnp.minimum((i + 1) * per, nblk - 1), col0 + h))

    def ctx(col0):
        return pl.BlockSpec((l_ctx, HEAD_DIM), lambda h, i: (0, col0 + h))

    return pl.pallas_call(
        _band_attn_kernel,
        grid=(N_KV_HEADS, n // tq),
        in_specs=[
            pl.BlockSpec(memory_space=pltpu.SMEM),
            pl.BlockSpec((tq, GROUP * HEAD_DIM), lambda h, i: (i, h)),
            prev(kcol), main(kcol), nxt(kcol), ctx(kcol),
            prev(vcol), main(vcol), nxt(vcol), ctx(vcol),
        ],
        out_specs=pl.BlockSpec((tq, GROUP * HEAD_DIM), lambda h, i: (i, h)),
        out_shape=jax.ShapeDtypeStruct((n, d_q), BF16),
        scratch_shapes=[pltpu.VMEM((tq + 2 * WINDOW, HEAD_DIM), BF16)] * 2,
        compiler_params=_params("arbitrary", "arbitrary"),
        name="band_attention",
    )(sink, qkv, qkv, qkv, qkv, qkv_c, qkv, qkv, qkv, qkv_c)


def _ctx_attn_kernel(sink_ref, q_ref, k_ref, v_ref, o_ref):
    h = pl.program_id(0)
    for g in range(GROUP):
        sl = slice(g * HEAD_DIM, (g + 1) * HEAD_DIM)
        o = _sink_softmax_pv(_scores2(q_ref[:, sl], k_ref[...]), v_ref[...], sink_ref[h * GROUP + g] * LOG2E)
        o_ref[:, sl] = o.astype(BF16)


def _ctx_attention(qkv_c, sink):
    l_ctx = qkv_c.shape[0]
    d_q = N_KV_HEADS * GROUP * HEAD_DIM
    kcol = d_q // HEAD_DIM
    vcol = kcol + N_KV_HEADS
    return pl.pallas_call(
        _ctx_attn_kernel,
        grid=(N_KV_HEADS,),
        in_specs=[
            pl.BlockSpec(memory_space=pltpu.SMEM),
            pl.BlockSpec((l_ctx, GROUP * HEAD_DIM), lambda h: (0, h)),
            pl.BlockSpec((l_ctx, HEAD_DIM), lambda h: (0, kcol + h)),
            pl.BlockSpec((l_ctx, HEAD_DIM), lambda h: (0, vcol + h)),
        ],
        out_specs=pl.BlockSpec((l_ctx, GROUP * HEAD_DIM), lambda h: (0, h)),
        out_shape=jax.ShapeDtypeStruct((l_ctx, d_q), BF16),
        compiler_params=_params("arbitrary"),
        name="ctx_attention",
    )(sink, qkv_c, qkv_c, qkv_c)


def _conv_in_kernel(h_ref, hp_ref, hn_ref, wb_ref, wc_ref, wv_ref, cw_ref, u_ref, hs_ref):
    @pl.when(pl.program_id(1) == 0)
    def _():
        _fill_halo_lhs(h_ref, hp_ref, hn_ref, hs_ref)

    tm = h_ref.shape[0]
    rc = min(ROW_CHUNK, tm)
    wb = wb_ref[...].astype(BF16)
    wc = wc_ref[...].astype(BF16)
    wv = wv_ref[...].astype(BF16)
    for c0 in range(0, tm, rc):
        hs = hs_ref[c0:c0 + rc + 2 * HALO_ROWS, :]
        p = _dot(hs, wc) * _dot(hs, wv)
        gate_b = _dot(h_ref[c0:c0 + rc, :], wb)
        u_ref[c0:c0 + rc, :] = (gate_b * _conv3_rows(p, rc, cw_ref)).astype(BF16)


def _conv_in(h, w_in, jl, conv_w, tm, tn=256):
    m, d = h.shape
    nt = d // tn

    def wspec(part):
        return pl.BlockSpec((None, d, tn), lambda i, j: (jl, 0, part * nt + j))

    return pl.pallas_call(
        _conv_in_kernel,
        grid=(m // tm, nt),
        in_specs=_halo_lhs_specs(tm, d, m) + [wspec(0), wspec(1), wspec(2),
                                              pl.BlockSpec((None, 3, tn), lambda i, j: (jl, 0, j))],
        out_specs=pl.BlockSpec((tm, tn), lambda i, j: (i, j)),
        out_shape=jax.ShapeDtypeStruct((m, d), BF16),
        scratch_shapes=[pltpu.VMEM((tm + 2 * HALO_ROWS, d), BF16)],
        compiler_params=_params("arbitrary", "arbitrary"),
        name="conv_in",
    )(h, h, h, w_in, w_in, w_in, conv_w)


def _ffn_up_kernel(h_ref, hp_ref, hn_ref, wg_ref, wv_ref, cw_ref, cb_ref, a_ref, hs_ref):
    @pl.when(pl.program_id(1) == 0)
    def _():
        _fill_halo_lhs(h_ref, hp_ref, hn_ref, hs_ref)

    tm = h_ref.shape[0]
    rc = min(ROW_CHUNK, tm)
    wg = wg_ref[...].astype(BF16)
    wv = wv_ref[...].astype(BF16)
    for c0 in range(0, tm, rc):
        gate = _conv3_rows(_dot(hs_ref[c0:c0 + rc + 2 * HALO_ROWS, :], wg), rc, cw_ref) + cb_ref[...]
        val = _dot(h_ref[c0:c0 + rc, :], wv)
        a_ref[c0:c0 + rc, :] = (jax.nn.silu(gate) * val).astype(BF16)


def _ffn_up(h, w_up, layer, conv_w, conv_b, tm, tn=256):
    m, d = h.shape
    f = w_up.shape[2] // 2
    nt = f // tn
    return pl.pallas_call(
        _ffn_up_kernel,
        grid=(m // tm, nt),
        in_specs=_halo_lhs_specs(tm, d, m) + [
            pl.BlockSpec((None, d, tn), lambda i, j: (layer, 0, j)),
            pl.BlockSpec((None, d, tn), lambda i, j: (layer, 0, nt + j)),
            pl.BlockSpec((None, 3, tn), lambda i, j: (layer, 0, j)),
            pl.BlockSpec((None, 1, tn), lambda i, j: (layer, 0, j)),
        ],
        out_specs=pl.BlockSpec((tm, tn), lambda i, j: (i, j)),
        out_shape=jax.ShapeDtypeStruct((m, f), BF16),
        scratch_shapes=[pltpu.VMEM((tm + 2 * HALO_ROWS, d), BF16)],
        compiler_params=_params("arbitrary", "arbitrary"),
        name="ffn_up",
    )(h, h, h, w_up, w_up, conv_w, conv_b.reshape(conv_b.shape[0], 1, f))


def _out_proj_kernel(*refs, row, has_next, n_k):
    if has_next:
        a_ref, w_ref, x_ref, g_ref, sh_ref, sc_ref, o_ref, hn_ref = refs
    else:
        a_ref, w_ref, x_ref, g_ref, o_ref = refs
    k = pl.program_id(1)
    last = n_k - 1
    tm = a_ref.shape[0]
    rc = min(OUT_ROW_CHUNK, tm)
    chunks = [slice(c0, c0 + rc) for c0 in range(0, tm, rc)]

    def finish(rows, acc):
        xn = x_ref[rows, :] + g_ref[row:row + 1, :] * acc
        o_ref[rows, :] = xn
        if has_next:
            hn_ref[rows, :] = _modulated(xn, sh_ref, sc_ref, row)

    if n_k == 1:
        for rows in chunks:
            finish(rows, _dot(a_ref[rows, :], w_ref[...]))
        return

    @pl.when(k == 0)
    def _():
        for rows in chunks:
            o_ref[rows, :] = _dot(a_ref[rows, :], w_ref[...])

    @pl.when((k > 0) & (k < last))
    def _():
        for rows in chunks:
            o_ref[rows, :] += _dot(a_ref[rows, :], w_ref[...])

    @pl.when(k == last)
    def _():
        for rows in chunks:
            finish(rows, o_ref[rows, :] + _dot(a_ref[rows, :], w_ref[...]))


def _out_proj(a, w_bf16, jl, x, mod, layer, gate_chunk, nxt, row, tm, tk=512):
    m, d = x.shape
    kdim = a.shape[1]
    in_specs = [
        pl.BlockSpec((tm, tk), lambda i, k: (i, k)),
        pl.BlockSpec((None, tk, d), lambda i, k: (jl, k, 0)),
        pl.BlockSpec((tm, d), lambda i, k: (i, 0)),
        _mod_spec(layer, gate_chunk, d),
    ]
    args = [a, w_bf16, x, mod]
    out_specs = [pl.BlockSpec((tm, d), lambda i, k: (i, 0))]
    out_shape = [jax.ShapeDtypeStruct((m, d), F32)]
    if nxt is not None:
        in_specs += [_mod_spec(nxt[0], nxt[1], d), _mod_spec(nxt[0], nxt[2], d)]
        args += [mod, mod]
        out_specs.append(pl.BlockSpec((tm, d), lambda i, k: (i, 0)))
        out_shape.append(jax.ShapeDtypeStruct((m, d), BF16))
    outs = pl.pallas_call(
        functools.partial(_out_proj_kernel, row=row, has_next=nxt is not None, n_k=kdim // tk),
        grid=(m // tm, kdim // tk),
        in_specs=in_specs,
        out_specs=out_specs,
        out_shape=out_shape,
        compiler_params=_params("arbitrary", "arbitrary"),
        name="out_proj",
    )(*args)
    return (outs[0], outs[1]) if nxt is not None else (outs[0], None)


def _out_proj_resident_kernel(*refs, row, has_next):
    if has_next:
        a_ref, w_ref, x_ref, g_ref, sh_ref, sc_ref, o_ref, hn_ref = refs
    else:
        a_ref, w_ref, x_ref, g_ref, o_ref = refs
    tm = a_ref.shape[0]
    rc = min(OUT_ROW_CHUNK, tm)
    for c0 in range(0, tm, rc):
        rows = slice(c0, c0 + rc)
        xn = x_ref[rows, :] + g_ref[row:row + 1, :] * _dot(a_ref[rows, :], w_ref[...])
        o_ref[rows, :] = xn
        if has_next:
            hn_ref[rows, :] = _modulated(xn, sh_ref, sc_ref, row)


def _out_proj_resident(a, w_bf16, jl, x, mod, layer, gate_chunk, nxt, row, tm):
    m, d = x.shape
    kdim = a.shape[1]
    in_specs = [
        pl.BlockSpec((tm, kdim), lambda i: (i, 0)),
        pl.BlockSpec((None, kdim, d), lambda i: (jl, 0, 0), pipeline_mode=pl.Buffered(1)),
        pl.BlockSpec((tm, d), lambda i: (i, 0)),
        _mod_spec(layer, gate_chunk, d),
    ]
    args = [a, w_bf16, x, mod]
    out_specs = [pl.BlockSpec((tm, d), lambda i: (i, 0))]
    out_shape = [jax.ShapeDtypeStruct((m, d), F32)]
    if nxt is not None:
        in_specs += [_mod_spec(nxt[0], nxt[1], d), _mod_spec(nxt[0], nxt[2], d)]
        args += [mod, mod]
        out_specs.append(pl.BlockSpec((tm, d), lambda i: (i, 0)))
        out_shape.append(jax.ShapeDtypeStruct((m, d), BF16))
    outs = pl.pallas_call(
        functools.partial(_out_proj_resident_kernel, row=row, has_next=nxt is not None),
        grid=(m // tm,),
        in_specs=in_specs,
        out_specs=out_specs,
        out_shape=out_shape,
        compiler_params=_params("arbitrary"),
        name="out_proj_resident",
    )(*args)
    return (outs[0], outs[1]) if nxt is not None else (outs[0], None)


def _rope_tables(n):
    rows = n // GRID_W
    row = jnp.repeat(jnp.arange(rows), GRID_W).astype(F32)
    col = jnp.tile(jnp.arange(GRID_W), rows).astype(F32)
    inv = ROPE_BASE ** (-jnp.arange(ROPE_PAIRS, dtype=F32) / ROPE_PAIRS)
    ang = jnp.stack([row[:, None] * inv, col[:, None] * inv], axis=1)
    ang = _paired_lanes(jnp.broadcast_to(ang[:, :, None, :], (n, 2, 2, ROPE_PAIRS)).reshape(n, HEAD_DIM), 1)
    sign = jnp.where(jnp.arange(HEAD_DIM) < HEAD_DIM // 2, -1.0, 1.0).astype(F32)
    return jnp.cos(ang), jnp.sin(ang) * sign


def kernel(x, c, ctx, c_ctx, w_ada, b_ada, attn_w_qkv, attn_w_o, attn_q_gain, attn_k_gain, attn_sink,
           sc_w_in, sc_conv, sc_w_out, ffn_w_up, ffn_conv, ffn_conv_b, ffn_w_down):
    batch, n, d = x.shape
    l_ctx = ctx.shape[1]
    depth = w_ada.shape[0]
    assert batch == 1 and d == N_KV_HEADS * GROUP * HEAD_DIM
    tm_in = min(2048, n)
    tm_out = min(1024, n)
    tm_res = min(512, n)
    tq = min(512, n)
    tm_c = l_ctx

    cc = jnp.concatenate([c, c_ctx[None, :], jnp.zeros((MOD_ROWS - 2, d), F32)], axis=0)
    mod = _ada_table(cc, w_ada, b_ada)
    rope_tabs = _rope_tables(n)
    n_qk_heads = N_KV_HEADS * GROUP + N_KV_HEADS
    w_qkv = jnp.concatenate([_paired_lanes(attn_w_qkv[..., :n_qk_heads * HEAD_DIM], n_qk_heads),
                             attn_w_qkv[..., n_qk_heads * HEAD_DIM:]], axis=-1).astype(BF16)
    w_o = attn_w_o.astype(BF16)
    w_out = sc_w_out.astype(BF16)
    w_down = ffn_w_down.astype(BF16)
    xs, cs = x[0], ctx[0]
    hx = _modulate(xs, mod, 0, ROW_LATENT, min(512, n))
    hc = _modulate(cs, mod, 0, ROW_CTX, tm_c)

    for l in range(depth):
        is_attn = (l % N_MIXERS) == 0
        j = l // N_MIXERS
        need_ctx = l < depth - 1
        nxt_ffn = (l, SHIFT_FFN, SCALE_FFN)
        nxt_mix = (l + 1, SHIFT_MIX, SCALE_MIX) if l + 1 < depth else None
        if is_attn:
            gains = _paired_lanes(jnp.stack([attn_q_gain[j], attn_k_gain[j]]), 1)
            qkv_c = _qkv_proj(hc, w_qkv, j, gains, None, tm_c)
            qkv = _qkv_proj(hx, w_qkv, j, gains, rope_tabs, tm_res)
            o = _band_attention(qkv, qkv_c, attn_sink[j], tq)
            xs, hx = _out_proj_resident(o, w_o, j, xs, mod, l, GATE_MIX, nxt_ffn, ROW_LATENT, tm_res)
            if need_ctx:
                o_c = _ctx_attention(qkv_c, attn_sink[j])
                cs, hc = _out_proj_resident(o_c, w_o, j, cs, mod, l, GATE_MIX, nxt_ffn, ROW_CTX, tm_c)
        else:
            u = _conv_in(hx, sc_w_in, j, sc_conv, tm_in)
            xs, hx = _out_proj_resident(u, w_out, j, xs, mod, l, GATE_MIX, nxt_ffn, ROW_LATENT, tm_res)
            if need_ctx:
                u_c = _conv_in(hc, sc_w_in, j, sc_conv, tm_c)
                cs, hc = _out_proj_resident(u_c, w_out, j, cs, mod, l, GATE_MIX, nxt_ffn, ROW_CTX, tm_c)
        a = _ffn_up(hx, ffn_w_up, l, ffn_conv, ffn_conv_b, tm_in)
        xs, hx = _out_proj(a, w_down, l, xs, mod, l, GATE_FFN, nxt_mix, ROW_LATENT, tm_out)
        if need_ctx:
            a_c = _ffn_up(hc, ffn_w_up, l, ffn_conv, ffn_conv_b, tm_c)
            cs, hc = _out_proj(a_c, w_down, l, cs, mod, l, GATE_FFN, nxt_mix, ROW_CTX, tm_c)
    return xs[None]
```

```python
import functools

import jax
import jax.numpy as jnp
from jax import lax
from jax.experimental import pallas as pl
from jax.experimental.pallas import tpu as pltpu

HEAD_DIM = 128
N_KV_HEADS = 4
GROUP = 4
WINDOW = 128
GRID_W = 64
ROPE_PAIRS = HEAD_DIM // 4
ROPE_BASE = 10000.0
N_MIXERS = 2
EPS = 1e-6
NEG_INF = -1e30

BF16 = jnp.bfloat16
F32 = jnp.float32

V7X_VMEM_LIMIT_BYTES = 56 * 1024 * 1024
HALO_ROWS = 16
MOD_ROWS = 8
ROW_LATENT = 0
ROW_CTX = 1
ROW_CHUNK = 512
OUT_ROW_CHUNK = 256
LOG2E = 1.4426950408889634
SCORE_SCALE2 = HEAD_DIM ** -0.5 * LOG2E
SHIFT_MIX, SCALE_MIX, GATE_MIX, SHIFT_FFN, SCALE_FFN, GATE_FFN = range(6)


def _params(*sem):
    return pltpu.CompilerParams(dimension_semantics=sem, vmem_limit_bytes=V7X_VMEM_LIMIT_BYTES)


def _dot(a, b):
    return jnp.dot(a, b, preferred_element_type=F32)


def _ada_kernel(cc_ref, w_ref, b_ref, o_ref):
    a = jax.nn.silu(cc_ref[...]).astype(BF16)
    o_ref[...] = _dot(a, w_ref[...].astype(BF16)) + b_ref[...]


def _ada_table(cc, w_ada, b_ada, tn=1024):
    depth, d, d6 = w_ada.shape
    return pl.pallas_call(
        _ada_kernel,
        grid=(depth, d6 // tn),
        in_specs=[
            pl.BlockSpec((MOD_ROWS, d), lambda l, j: (0, 0)),
            pl.BlockSpec((None, d, tn), lambda l, j: (l, 0, j)),
            pl.BlockSpec((None, 1, tn), lambda l, j: (l, 0, j)),
        ],
        out_specs=pl.BlockSpec((None, MOD_ROWS, tn), lambda l, j: (l, 0, j)),
        out_shape=jax.ShapeDtypeStruct((depth, MOD_ROWS, d6), F32),
        compiler_params=_params("arbitrary", "arbitrary"),
        name="ada_table",
    )(cc, w_ada, b_ada.reshape(depth, 1, d6))


def _modulated(x, sh_ref, sc_ref, row):
    r = lax.rsqrt(jnp.mean(x * x, axis=-1, keepdims=True) + EPS)
    return ((x * r) * (1 + sc_ref[row:row + 1, :]) + sh_ref[row:row + 1, :]).astype(BF16)


def _mod_spec(layer, chunk, d):
    return pl.BlockSpec((None, MOD_ROWS, d), lambda *_: (layer, 0, chunk))


def _modulate_kernel(x_ref, sh_ref, sc_ref, h_ref, *, row):
    h_ref[...] = _modulated(x_ref[...], sh_ref, sc_ref, row)


def _modulate(x, mod, layer, row, tm):
    m, d = x.shape
    return pl.pallas_call(
        functools.partial(_modulate_kernel, row=row),
        grid=(m // tm,),
        in_specs=[pl.BlockSpec((tm, d), lambda i: (i, 0)), _mod_spec(layer, SHIFT_MIX, d), _mod_spec(layer, SCALE_MIX, d)],
        out_specs=pl.BlockSpec((tm, d), lambda i: (i, 0)),
        out_shape=jax.ShapeDtypeStruct((m, d), BF16),
        compiler_params=_params("arbitrary"),
        name="modulate",
    )(x, mod, mod)


def _fill_halo_lhs(h_ref, hp_ref, hn_ref, hs_ref):
    i = pl.program_id(0)
    tm = h_ref.shape[0]
    zero = jnp.zeros(hp_ref.shape, BF16)
    hs_ref[0:HALO_ROWS, :] = jnp.where(i > 0, hp_ref[...], zero)
    hs_ref[HALO_ROWS:HALO_ROWS + tm, :] = h_ref[...]
    hs_ref[HALO_ROWS + tm:, :] = jnp.where(i < pl.num_programs(0) - 1, hn_ref[...], zero)


def _halo_lhs_specs(tm, d, m):
    per = tm // HALO_ROWS
    nblk = m // HALO_ROWS
    return [
        pl.BlockSpec((tm, d), lambda i, j: (i, 0)),
        pl.BlockSpec((HALO_ROWS, d), lambda i, j: (jnp.maximum(i * per - 1, 0), 0)),
        pl.BlockSpec((HALO_ROWS, d), lambda i, j: (jnp.minimum((i + 1) * per, nblk - 1), 0)),
    ]


def _row_chunks(tm):
    rc = min(ROW_CHUNK, tm)
    return [(c0, rc) for c0 in range(0, tm, rc)]


def _conv3_rows(y, tm, w_ref):
    ext = y.shape[0]
    sl = slice(HALO_ROWS, HALO_ROWS + tm)
    dn = pltpu.roll(y, 1, 0)[sl]
    up = pltpu.roll(y, ext - 1, 0)[sl]
    return dn * w_ref[0:1, :] + y[sl] * w_ref[1:2, :] + up * w_ref[2:3, :]


def _paired_lanes(a, n_heads):
    lead = a.shape[:-1]
    return a.reshape(*lead, n_heads, 2, 2, ROPE_PAIRS).swapaxes(-3, -2).reshape(*lead, n_heads * HEAD_DIM)


def _pair_weight_kernel(w_ref, p_ref, o_ref):
    o_ref[...] = _dot(w_ref[...].astype(BF16), p_ref[...]).astype(BF16)


def _paired_qkv_weight(w_qkv, n_qk_cols, tn=512):
    layers, d, n_out = w_qkv.shape
    eye = jnp.eye(tn, dtype=BF16)
    perms = jnp.stack([_paired_lanes(eye, tn // HEAD_DIM), eye])
    n_qk_tiles = n_qk_cols // tn
    return pl.pallas_call(
        _pair_weight_kernel,
        grid=(layers, n_out // tn),
        in_specs=[
            pl.BlockSpec((None, d, tn), lambda l, j: (l, 0, j)),
            pl.BlockSpec((None, tn, tn), lambda l, j: (jnp.where(j < n_qk_tiles, 0, 1), 0, 0)),
        ],
        out_specs=pl.BlockSpec((None, d, tn), lambda l, j: (l, 0, j)),
        out_shape=jax.ShapeDtypeStruct((layers, d, n_out), BF16),
        compiler_params=_params("arbitrary", "arbitrary"),
        name="pair_qkv_weight",
    )(w_qkv, perms)


def _qkv_kernel(*refs, rope, tn, n_q_tiles, n_qk_tiles):
    if rope:
        h_ref, w_ref, gain_ref, cos_ref, sin_ref, o_ref = refs
    else:
        h_ref, w_ref, gain_ref, o_ref = refs
    h = h_ref[...]
    for jt in range(w_ref.shape[1] // tn):
        y = _dot(h, w_ref[:, jt * tn:(jt + 1) * tn])
        if jt >= n_qk_tiles:
            o_ref[:, jt * tn:(jt + 1) * tn] = y.astype(BF16)
            continue
        gain = gain_ref[0:1, :] * SCORE_SCALE2 if jt < n_q_tiles else gain_ref[1:2, :]
        for hh in range(tn // HEAD_DIM):
            yh = y[:, hh * HEAD_DIM:(hh + 1) * HEAD_DIM]
            r = lax.rsqrt(jnp.mean(yh * yh, axis=-1, keepdims=True) + EPS)
            yh = (yh * r) * gain
            if rope:
                yh = yh * cos_ref[...] + pltpu.roll(yh, HEAD_DIM // 2, 1) * sin_ref[...]
            o_ref[:, jt * tn + hh * HEAD_DIM:jt * tn + (hh + 1) * HEAD_DIM] = yh.astype(BF16)


def _qkv_proj(h, w_qkv_paired, jl, gains_paired, rope_tabs, tm, tn=512):
    m, d = h.shape
    n_out = w_qkv_paired.shape[2]
    d_q = d
    d_kv = (n_out - d_q) // 2
    rope = rope_tabs is not None
    in_specs = [
        pl.BlockSpec((tm, d), lambda i: (i, 0)),
        pl.BlockSpec((None, d, n_out), lambda i: (jl, 0, 0), pipeline_mode=pl.Buffered(1)),
        pl.BlockSpec((2, HEAD_DIM), lambda i: (0, 0)),
    ]
    args = [h, w_qkv_paired, gains_paired]
    if rope:
        in_specs += [pl.BlockSpec((tm, HEAD_DIM), lambda i: (i, 0))] * 2
        args += list(rope_tabs)
    kern = functools.partial(_qkv_kernel, rope=rope, tn=tn, n_q_tiles=d_q // tn, n_qk_tiles=(d_q + d_kv) // tn)
    return pl.pallas_call(
        kern,
        grid=(m // tm,),
        in_specs=in_specs,
        out_specs=pl.BlockSpec((tm, n_out), lambda i: (i, 0)),
        out_shape=jax.ShapeDtypeStruct((m, n_out), BF16),
        compiler_params=_params("arbitrary"),
        name="qkv_proj",
    )(*args)


def _sink_softmax(t, sink2):
    rows, n_keys = t.shape
    wide = (rows, HEAD_DIM)
    m2 = jnp.maximum(jnp.broadcast_to(jnp.max(t, axis=-1, keepdims=True), wide), sink2)
    e = jnp.exp2(t - jnp.tile(m2, (1, n_keys // HEAD_DIM)))
    den = jnp.broadcast_to(jnp.sum(e, axis=-1, keepdims=True), wide) + jnp.exp2(sink2 - m2)
    return e.astype(BF16), den


def _scores2(q, k):
    return lax.dot_general(q, k, (((1,), (1,)), ((), ())), preferred_element_type=F32)


def _band_attn_kernel(sink_ref, q_ref, kp_ref, km_ref, kn_ref, kc_ref, vp_ref, vm_ref, vn_ref, vc_ref, o_ref,
                      kw_ref, vw_ref):
    h = pl.program_id(0)
    i = pl.program_id(1)
    tq = q_ref.shape[0]
    nb = tq // WINDOW
    last_blk = pl.num_programs(1) * nb - 1
    kw_ref[0:WINDOW, :] = kp_ref[...]
    kw_ref[WINDOW:WINDOW + tq, :] = km_ref[...]
    kw_ref[WINDOW + tq:, :] = kn_ref[...]
    vw_ref[0:WINDOW, :] = vp_ref[...]
    vw_ref[WINDOW:WINDOW + tq, :] = vm_ref[...]
    vw_ref[WINDOW + tq:, :] = vn_ref[...]
    rows = GROUP * WINDOW
    r = lax.broadcasted_iota(jnp.int32, (rows, WINDOW), 0) & (WINDOW - 1)
    c_minus_r = lax.broadcasted_iota(jnp.int32, (rows, WINDOW), 1) - r
    sink2 = jnp.concatenate([jnp.full((WINDOW, HEAD_DIM), sink_ref[h * GROUP + g] * LOG2E, F32)
                             for g in range(GROUP)], axis=0)
    kc = kc_ref[...]
    vc = vc_ref[...]

    def masked_scores(b):
        blk = i * nb + b
        q4 = jnp.concatenate([q_ref[b * WINDOW:(b + 1) * WINDOW, g * HEAD_DIM:(g + 1) * HEAD_DIM]
                              for g in range(GROUP)], axis=0)
        s = _scores2(q4, jnp.concatenate([kw_ref[b * WINDOW:(b + 3) * WINDOW, :], kc], axis=0))
        lo = jnp.where(blk > 0, 0, WINDOW)
        hi = jnp.where(blk < last_blk, 0, -WINDOW)
        s_prev = jnp.where(c_minus_r >= lo, s[:, 0:WINDOW], NEG_INF)
        s_next = jnp.where(c_minus_r <= hi, s[:, 2 * WINDOW:3 * WINDOW], NEG_INF)
        return jnp.concatenate([s_prev, s[:, WINDOW:2 * WINDOW], s_next, s[:, 3 * WINDOW:]], axis=1)

    def weighted_values(b, e, den):
        o = _dot(e, jnp.concatenate([vw_ref[b * WINDOW:(b + 3) * WINDOW, :], vc], axis=0)) / den
        for g in range(GROUP):
            o_ref[b * WINDOW:(b + 1) * WINDOW, g * HEAD_DIM:(g + 1) * HEAD_DIM] = (
                o[g * WINDOW:(g + 1) * WINDOW, :].astype(BF16))

    for b in range(nb):
        weighted_values(b, *_sink_softmax(masked_scores(b), sink2))


def _band_attention(qkv, qkv_c, sink, tq):
    n = qkv.shape[0]
    l_ctx = qkv_c.shape[0]
    d_q = N_KV_HEADS * GROUP * HEAD_DIM
    kcol = d_q // HEAD_DIM
    vcol = kcol + N_KV_HEADS
    per = tq // WINDOW
    nblk = n // WINDOW

    def prev(col0):
        return pl.BlockSpec((WINDOW, HEAD_DIM), lambda h, i: (jnp.maximum(i * per - 1, 0), col0 + h))

    def main(col0):
        return pl.BlockSpec((tq, HEAD_DIM), lambda h, i: (i, col0 + h))

    def nxt(col0):
        return pl.BlockSpec((WINDOW, HEAD_DIM), lambda h, i: (jnp.minimum((i + 1) * per, nblk - 1), col0 + h))

    def ctx(col0):
        return pl.BlockSpec((l_ctx, HEAD_DIM), lambda h, i: (0, col0 + h))

    return pl.pallas_call(
        _band_attn_kernel,
        grid=(N_KV_HEADS, n // tq),
        in_specs=[
            pl.BlockSpec(memory_space=pltpu.SMEM),
            pl.BlockSpec((tq, GROUP * HEAD_DIM), lambda h, i: (i, h)),
            prev(kcol), main(kcol), nxt(kcol), ctx(kcol),
            prev(vcol), main(vcol), nxt(vcol), ctx(vcol),
        ],
        out_specs=pl.BlockSpec((tq, GROUP * HEAD_DIM), lambda h, i: (i, h)),
        out_shape=jax.ShapeDtypeStruct((n, d_q), BF16),
        scratch_shapes=[pltpu.VMEM((tq + 2 * WINDOW, HEAD_DIM), BF16)] * 2,
        compiler_params=_params("arbitrary", "arbitrary"),
        name="band_attention",
    )(sink, qkv, qkv, qkv, qkv, qkv_c, qkv, qkv, qkv, qkv_c)


def _ctx_attn_kernel(sink_ref, q_ref, k_ref, v_ref, o_ref):
    h = pl.program_id(0)
    for g in range(GROUP):
        sl = slice(g * HEAD_DIM, (g + 1) * HEAD_DIM)
        sink2 = jnp.full((q_ref.shape[0], HEAD_DIM), sink_ref[h * GROUP + g] * LOG2E, F32)
        e, den = _sink_softmax(_scores2(q_ref[:, sl], k_ref[...]), sink2)
        o = _dot(e, v_ref[...]) / den
        o_ref[:, sl] = o.astype(BF16)


def _ctx_attention(qkv_c, sink):
    l_ctx = qkv_c.shape[0]
    d_q = N_KV_HEADS * GROUP * HEAD_DIM
    kcol = d_q // HEAD_DIM
    vcol = kcol + N_KV_HEADS
    return pl.pallas_call(
        _ctx_attn_kernel,
        grid=(N_KV_HEADS,),
        in_specs=[
            pl.BlockSpec(memory_space=pltpu.SMEM),
            pl.BlockSpec((l_ctx, GROUP * HEAD_DIM), lambda h: (0, h)),
            pl.BlockSpec((l_ctx, HEAD_DIM), lambda h: (0, kcol + h)),
            pl.BlockSpec((l_ctx, HEAD_DIM), lambda h: (0, vcol + h)),
        ],
        out_specs=pl.BlockSpec((l_ctx, GROUP * HEAD_DIM), lambda h: (0, h)),
        out_shape=jax.ShapeDtypeStruct((l_ctx, d_q), BF16),
        compiler_params=_params("arbitrary"),
        name="ctx_attention",
    )(sink, qkv_c, qkv_c, qkv_c)


def _conv_in_kernel(h_ref, hp_ref, hn_ref, wb_ref, wc_ref, wv_ref, cw_ref, u_ref, hs_ref):
    @pl.when(pl.program_id(1) == 0)
    def _():
        _fill_halo_lhs(h_ref, hp_ref, hn_ref, hs_ref)

    wb = wb_ref[...].astype(BF16)
    wc = wc_ref[...].astype(BF16)
    wv = wv_ref[...].astype(BF16)
    for c0, rc in _row_chunks(h_ref.shape[0]):
        hs = hs_ref[c0:c0 + rc + 2 * HALO_ROWS, :]
        p = _dot(hs, wc) * _dot(hs, wv)
        gate_b = _dot(hs_ref[c0 + HALO_ROWS:c0 + HALO_ROWS + rc, :], wb)
        u_ref[c0:c0 + rc, :] = (gate_b * _conv3_rows(p, rc, cw_ref)).astype(BF16)


def _conv_in(h, w_in, jl, conv_w, tm, tn=256):
    m, d = h.shape
    nt = d // tn

    def wspec(part):
        return pl.BlockSpec((None, d, tn), lambda i, j: (jl, 0, part * nt + j))

    return pl.pallas_call(
        _conv_in_kernel,
        grid=(m // tm, nt),
        in_specs=_halo_lhs_specs(tm, d, m) + [wspec(0), wspec(1), wspec(2),
                                              pl.BlockSpec((None, 3, tn), lambda i, j: (jl, 0, j))],
        out_specs=pl.BlockSpec((tm, tn), lambda i, j: (i, j)),
        out_shape=jax.ShapeDtypeStruct((m, d), BF16),
        scratch_shapes=[pltpu.VMEM((tm + 2 * HALO_ROWS, d), BF16)],
        compiler_params=_params("arbitrary", "arbitrary"),
        name="conv_in",
    )(h, h, h, w_in, w_in, w_in, conv_w)


def _ffn_up_kernel(h_ref, hp_ref, hn_ref, wg_ref, wv_ref, cw_ref, cb_ref, a_ref, hs_ref):
    @pl.when(pl.program_id(1) == 0)
    def _():
        _fill_halo_lhs(h_ref, hp_ref, hn_ref, hs_ref)

    wg = wg_ref[...].astype(BF16)
    wv = wv_ref[...].astype(BF16)
    for c0, rc in _row_chunks(h_ref.shape[0]):
        gate = _conv3_rows(_dot(hs_ref[c0:c0 + rc + 2 * HALO_ROWS, :], wg), rc, cw_ref) + cb_ref[...]
        val = _dot(hs_ref[c0 + HALO_ROWS:c0 + HALO_ROWS + rc, :], wv)
        a_ref[c0:c0 + rc, :] = (jax.nn.silu(gate) * val).astype(BF16)


def _ffn_up(h, w_up, layer, conv_w, conv_b, tm, tn=256):
    m, d = h.shape
    f = w_up.shape[2] // 2
    nt = f // tn
    return pl.pallas_call(
        _ffn_up_kernel,
        grid=(m // tm, nt),
        in_specs=_halo_lhs_specs(tm, d, m) + [
            pl.BlockSpec((None, d, tn), lambda i, j: (layer, 0, j)),
            pl.BlockSpec((None, d, tn), lambda i, j: (layer, 0, nt + j)),
            pl.BlockSpec((None, 3, tn), lambda i, j: (layer, 0, j)),
            pl.BlockSpec((None, 1, tn), lambda i, j: (layer, 0, j)),
        ],
        out_specs=pl.BlockSpec((tm, tn), lambda i, j: (i, j)),
        out_shape=jax.ShapeDtypeStruct((m, f), BF16),
        scratch_shapes=[pltpu.VMEM((tm + 2 * HALO_ROWS, d), BF16)],
        compiler_params=_params("arbitrary", "arbitrary"),
        name="ffn_up",
    )(h, h, h, w_up, w_up, conv_w, conv_b.reshape(conv_b.shape[0], 1, f))


def _out_proj_kernel(*refs, row, has_next, n_k):
    if has_next:
        a_ref, w_ref, x_ref, g_ref, sh_ref, sc_ref, o_ref, hn_ref = refs
    else:
        a_ref, w_ref, x_ref, g_ref, o_ref = refs
    k = pl.program_id(1)
    last = n_k - 1
    tm = a_ref.shape[0]
    rc = min(OUT_ROW_CHUNK, tm)
    chunks = [slice(c0, c0 + rc) for c0 in range(0, tm, rc)]

    def finish(rows, acc):
        xn = x_ref[rows, :] + g_ref[row:row + 1, :] * acc
        o_ref[rows, :] = xn
        if has_next:
            hn_ref[rows, :] = _modulated(xn, sh_ref, sc_ref, row)

    if n_k == 1:
        for rows in chunks:
            finish(rows, _dot(a_ref[rows, :], w_ref[...]))
        return

    @pl.when(k == 0)
    def _():
        for rows in chunks:
            o_ref[rows, :] = _dot(a_ref[rows, :], w_ref[...])

    @pl.when((k > 0) & (k < last))
    def _():
        for rows in chunks:
            o_ref[rows, :] += _dot(a_ref[rows, :], w_ref[...])

    @pl.when(k == last)
    def _():
        for rows in chunks:
            finish(rows, o_ref[rows, :] + _dot(a_ref[rows, :], w_ref[...]))


def _out_proj(a, w_bf16, jl, x, mod, layer, gate_chunk, nxt, row, tm, tk=512):
    m, d = x.shape
    kdim = a.shape[1]
    in_specs = [
        pl.BlockSpec((tm, tk), lambda i, k: (i, k)),
        pl.BlockSpec((None, tk, d), lambda i, k: (jl, k, 0)),
        pl.BlockSpec((tm, d), lambda i, k: (i, 0)),
        _mod_spec(layer, gate_chunk, d),
    ]
    args = [a, w_bf16, x, mod]
    out_specs = [pl.BlockSpec((tm, d), lambda i, k: (i, 0))]
    out_shape = [jax.ShapeDtypeStruct((m, d), F32)]
    if nxt is not None:
        in_specs += [_mod_spec(nxt[0], nxt[1], d), _mod_spec(nxt[0], nxt[2], d)]
        args += [mod, mod]
        out_specs.append(pl.BlockSpec((tm, d), lambda i, k: (i, 0)))
        out_shape.append(jax.ShapeDtypeStruct((m, d), BF16))
    outs = pl.pallas_call(
        functools.partial(_out_proj_kernel, row=row, has_next=nxt is not None, n_k=kdim // tk),
        grid=(m // tm, kdim // tk),
        in_specs=in_specs,
        out_specs=out_specs,
        out_shape=out_shape,
        compiler_params=_params("arbitrary", "arbitrary"),
        name="out_proj",
    )(*args)
    return (outs[0], outs[1]) if nxt is not None else (outs[0], None)


def _out_proj_resident_kernel(*refs, row, has_next):
    if has_next:
        a_ref, w_ref, x_ref, g_ref, sh_ref, sc_ref, o_ref, hn_ref = refs
    else:
        a_ref, w_ref, x_ref, g_ref, o_ref = refs
    tm = a_ref.shape[0]
    rc = min(OUT_ROW_CHUNK, tm)
    for c0 in range(0, tm, rc):
        rows = slice(c0, c0 + rc)
        xn = x_ref[rows, :] + g_ref[row:row + 1, :] * _dot(a_ref[rows, :], w_ref[...])
        o_ref[rows, :] = xn
        if has_next:
            hn_ref[rows, :] = _modulated(xn, sh_ref, sc_ref, row)


def _out_proj_resident(a, w_bf16, jl, x, mod, layer, gate_chunk, nxt, row, tm):
    m, d = x.shape
    kdim = a.shape[1]
    in_specs = [
        pl.BlockSpec((tm, kdim), lambda i: (i, 0)),
        pl.BlockSpec((None, kdim, d), lambda i: (jl, 0, 0), pipeline_mode=pl.Buffered(1)),
        pl.BlockSpec((tm, d), lambda i: (i, 0)),
        _mod_spec(layer, gate_chunk, d),
    ]
    args = [a, w_bf16, x, mod]
    out_specs = [pl.BlockSpec((tm, d), lambda i: (i, 0))]
    out_shape = [jax.ShapeDtypeStruct((m, d), F32)]
    if nxt is not None:
        in_specs += [_mod_spec(nxt[0], nxt[1], d), _mod_spec(nxt[0], nxt[2], d)]
        args += [mod, mod]
        out_specs.append(pl.BlockSpec((tm, d), lambda i: (i, 0)))
        out_shape.append(jax.ShapeDtypeStruct((m, d), BF16))
    outs = pl.pallas_call(
        functools.partial(_out_proj_resident_kernel, row=row, has_next=nxt is not None),
        grid=(m // tm,),
        in_specs=in_specs,
        out_specs=out_specs,
        out_shape=out_shape,
        compiler_params=_params("arbitrary"),
        name="out_proj_resident",
    )(*args)
    return (outs[0], outs[1]) if nxt is not None else (outs[0], None)


def _rope_tables(n):
    rows = n // GRID_W
    inv = ROPE_BASE ** (-jnp.arange(ROPE_PAIRS, dtype=F32) / ROPE_PAIRS)
    row_ang = jnp.arange(rows, dtype=F32)[:, None] * inv
    col_ang = jnp.arange(GRID_W, dtype=F32)[:, None] * inv

    def per_token(row_tab, col_tab, lo_sign):
        r = jnp.repeat(row_tab, GRID_W, axis=0)
        c = jnp.tile(col_tab, (rows, 1))
        return jnp.concatenate([lo_sign * r, lo_sign * c, r, c], axis=1)

    return per_token(jnp.cos(row_ang), jnp.cos(col_ang), 1.0), per_token(jnp.sin(row_ang), jnp.sin(col_ang), -1.0)


def kernel(x, c, ctx, c_ctx, w_ada, b_ada, attn_w_qkv, attn_w_o, attn_q_gain, attn_k_gain, attn_sink,
           sc_w_in, sc_conv, sc_w_out, ffn_w_up, ffn_conv, ffn_conv_b, ffn_w_down):
    batch, n, d = x.shape
    l_ctx = ctx.shape[1]
    depth = w_ada.shape[0]
    assert batch == 1 and d == N_KV_HEADS * GROUP * HEAD_DIM
    tm_in = min(2048, n)
    tm_out = min(1024, n)
    tm_res = min(512, n)
    tq = min(1024, n)
    tm_c = l_ctx

    cc = jnp.concatenate([c, c_ctx[None, :], jnp.zeros((MOD_ROWS - 2, d), F32)], axis=0)
    mod = _ada_table(cc, w_ada, b_ada)
    rope_tabs = _rope_tables(n)
    n_qk_heads = N_KV_HEADS * GROUP + N_KV_HEADS
    w_qkv = _paired_qkv_weight(attn_w_qkv, n_qk_heads * HEAD_DIM)
    w_o = attn_w_o.astype(BF16)
    w_out = sc_w_out.astype(BF16)
    w_down = ffn_w_down.astype(BF16)
    xs, cs = x[0], ctx[0]
    hx = _modulate(xs, mod, 0, ROW_LATENT, min(512, n))
    hc = _modulate(cs, mod, 0, ROW_CTX, tm_c)

    for l in range(depth):
        is_attn = (l % N_MIXERS) == 0
        j = l // N_MIXERS
        need_ctx = l < depth - 1
        nxt_ffn = (l, SHIFT_FFN, SCALE_FFN)
        nxt_mix = (l + 1, SHIFT_MIX, SCALE_MIX) if l + 1 < depth else None
        if is_attn:
            gains = _paired_lanes(jnp.stack([attn_q_gain[j], attn_k_gain[j]]), 1)
            qkv_c = _qkv_proj(hc, w_qkv, j, gains, None, tm_c)
            qkv = _qkv_proj(hx, w_qkv, j, gains, rope_tabs, tm_res)
            o = _band_attention(qkv, qkv_c, attn_sink[j], tq)
            xs, hx = _out_proj_resident(o, w_o, j, xs, mod, l, GATE_MIX, nxt_ffn, ROW_LATENT, tm_res)
            if need_ctx:
                o_c = _ctx_attention(qkv_c, attn_sink[j])
                cs, hc = _out_proj_resident(o_c, w_o, j, cs, mod, l, GATE_MIX, nxt_ffn, ROW_CTX, tm_c)
        else:
            u = _conv_in(hx, sc_w_in, j, sc_conv, tm_in)
            xs, hx = _out_proj_resident(u, w_out, j, xs, mod, l, GATE_MIX, nxt_ffn, ROW_LATENT, tm_res)
            if need_ctx:
                u_c = _conv_in(hc, sc_w_in, j, sc_conv, tm_c)
                cs, hc = _out_proj_resident(u_c, w_out, j, cs, mod, l, GATE_MIX, nxt_ffn, ROW_CTX, tm_c)
        a = _ffn_up(hx, ffn_w_up, l, ffn_conv, ffn_conv_b, tm_in)
        xs, hx = _out_proj(a, w_down, l, xs, mod, l, GATE_FFN, nxt_mix, ROW_LATENT, tm_out)
        if need_ctx:
            a_c = _ffn_up(hc, ffn_w_up, l, ffn_conv, ffn_conv_b, tm_c)
            cs, hc = _out_proj(a_c, w_down, l, cs, mod, l, GATE_FFN, nxt_mix, ROW_CTX, tm_c)
    return xs[None]
```

```python
import functools

import jax
import jax.numpy as jnp
from jax import lax
from jax.experimental import pallas as pl
from jax.experimental.pallas import tpu as pltpu

HEAD_DIM = 128
N_KV_HEADS = 4
GROUP = 4
WINDOW = 128
GRID_W = 64
ROPE_PAIRS = HEAD_DIM // 4
ROPE_BASE = 10000.0
N_MIXERS = 2
EPS = 1e-6
NEG_INF = -1e30

BF16 = jnp.bfloat16
F32 = jnp.float32

V7X_VMEM_LIMIT_BYTES = 56 * 1024 * 1024
HALO_ROWS = 16
MOD_ROWS = 8
ROW_LATENT = 0
ROW_CTX = 1
ROW_CHUNK = 512
OUT_ROW_CHUNK = 256
LOG2E = 1.4426950408889634
SCORE_SCALE2 = HEAD_DIM ** -0.5 * LOG2E
SHIFT_MIX, SCALE_MIX, GATE_MIX, SHIFT_FFN, SCALE_FFN, GATE_FFN = range(6)


def _params(*sem):
    return pltpu.CompilerParams(dimension_semantics=sem, vmem_limit_bytes=V7X_VMEM_LIMIT_BYTES)


def _dot(a, b):
    return jnp.dot(a, b, preferred_element_type=F32)


def _ada_kernel(cc_ref, w_ref, b_ref, o_ref):
    a = jax.nn.silu(cc_ref[...]).astype(BF16)
    o_ref[...] = _dot(a, w_ref[...].astype(BF16)) + b_ref[...]


def _ada_table(cc, w_ada, b_ada, tn=1024):
    depth, d, d6 = w_ada.shape
    return pl.pallas_call(
        _ada_kernel,
        grid=(depth, d6 // tn),
        in_specs=[
            pl.BlockSpec((MOD_ROWS, d), lambda l, j: (0, 0)),
            pl.BlockSpec((None, d, tn), lambda l, j: (l, 0, j)),
            pl.BlockSpec((None, 1, tn), lambda l, j: (l, 0, j)),
        ],
        out_specs=pl.BlockSpec((None, MOD_ROWS, tn), lambda l, j: (l, 0, j)),
        out_shape=jax.ShapeDtypeStruct((depth, MOD_ROWS, d6), F32),
        compiler_params=_params("arbitrary", "arbitrary"),
        name="ada_table",
    )(cc, w_ada, b_ada.reshape(depth, 1, d6))


def _modulated(x, sh_ref, sc_ref, row):
    r = lax.rsqrt(jnp.mean(x * x, axis=-1, keepdims=True) + EPS)
    return ((x * r) * (1 + sc_ref[row:row + 1, :]) + sh_ref[row:row + 1, :]).astype(BF16)


def _mod_spec(layer, chunk, d):
    return pl.BlockSpec((None, MOD_ROWS, d), lambda *_: (layer, 0, chunk))


def _modulate_kernel(x_ref, sh_ref, sc_ref, h_ref, *, row):
    h_ref[...] = _modulated(x_ref[...], sh_ref, sc_ref, row)


def _modulate(x, mod, layer, row, tm):
    m, d = x.shape
    return pl.pallas_call(
        functools.partial(_modulate_kernel, row=row),
        grid=(m // tm,),
        in_specs=[pl.BlockSpec((tm, d), lambda i: (i, 0)), _mod_spec(layer, SHIFT_MIX, d), _mod_spec(layer, SCALE_MIX, d)],
        out_specs=pl.BlockSpec((tm, d), lambda i: (i, 0)),
        out_shape=jax.ShapeDtypeStruct((m, d), BF16),
        compiler_params=_params("arbitrary"),
        name="modulate",
    )(x, mod, mod)


def _fill_halo_lhs(h_ref, hp_ref, hn_ref, hs_ref):
    i = pl.program_id(0)
    tm = h_ref.shape[0]
    zero = jnp.zeros(hp_ref.shape, BF16)
    hs_ref[0:HALO_ROWS, :] = jnp.where(i > 0, hp_ref[...], zero)
    hs_ref[HALO_ROWS:HALO_ROWS + tm, :] = h_ref[...]
    hs_ref[HALO_ROWS + tm:, :] = jnp.where(i < pl.num_programs(0) - 1, hn_ref[...], zero)


def _halo_lhs_specs(tm, d, m):
    per = tm // HALO_ROWS
    nblk = m // HALO_ROWS
    return [
        pl.BlockSpec((tm, d), lambda i, j: (i, 0)),
        pl.BlockSpec((HALO_ROWS, d), lambda i, j: (jnp.maximum(i * per - 1, 0), 0)),
        pl.BlockSpec((HALO_ROWS, d), lambda i, j: (jnp.minimum((i + 1) * per, nblk - 1), 0)),
    ]


class _SideCast:
    def __init__(self, w, layer, n_steps, step_of):
        _, kdim, d = w.shape
        rows = kdim // n_steps
        assert kdim % n_steps == 0 and rows % HALO_ROWS == 0
        self.arg = w
        self.in_spec = pl.BlockSpec((None, rows, d), lambda *g: (layer, step_of(*g), 0))
        self.out_spec = pl.BlockSpec((rows, d), lambda *g: (step_of(*g), 0))
        self.out_shape = jax.ShapeDtypeStruct((kdim, d), BF16)

    @staticmethod
    def fits(w, n_steps):
        kdim = w.shape[1]
        return kdim % n_steps == 0 and (kdim // n_steps) % HALO_ROWS == 0


def _row_chunks(tm):
    rc = min(ROW_CHUNK, tm)
    return [(c0, rc) for c0 in range(0, tm, rc)]


def _conv3_rows(y, tm, w_ref):
    ext = y.shape[0]
    sl = slice(HALO_ROWS, HALO_ROWS + tm)
    dn = pltpu.roll(y, 1, 0)[sl]
    up = pltpu.roll(y, ext - 1, 0)[sl]
    return dn * w_ref[0:1, :] + y[sl] * w_ref[1:2, :] + up * w_ref[2:3, :]


def _paired_lanes(a, n_heads):
    lead = a.shape[:-1]
    return a.reshape(*lead, n_heads, 2, 2, ROPE_PAIRS).swapaxes(-3, -2).reshape(*lead, n_heads * HEAD_DIM)


def _pair_weight_kernel(w_ref, p_ref, o_ref):
    o_ref[...] = _dot(w_ref[...].astype(BF16), p_ref[...]).astype(BF16)


def _paired_qkv_weight(w_qkv, n_qk_cols, tn=512):
    layers, d, n_out = w_qkv.shape
    eye = jnp.eye(tn, dtype=BF16)
    perms = jnp.stack([_paired_lanes(eye, tn // HEAD_DIM), eye])
    n_qk_tiles = n_qk_cols // tn
    return pl.pallas_call(
        _pair_weight_kernel,
        grid=(layers, n_out // tn),
        in_specs=[
            pl.BlockSpec((None, d, tn), lambda l, j: (l, 0, j)),
            pl.BlockSpec((None, tn, tn), lambda l, j: (jnp.where(j < n_qk_tiles, 0, 1), 0, 0)),
        ],
        out_specs=pl.BlockSpec((None, d, tn), lambda l, j: (l, 0, j)),
        out_shape=jax.ShapeDtypeStruct((layers, d, n_out), BF16),
        compiler_params=_params("arbitrary", "arbitrary"),
        name="pair_qkv_weight",
    )(w_qkv, perms)


def _qkv_kernel(*refs, rope, side, tn, n_q_tiles, n_qk_tiles):
    if side:
        *refs, side_out_ref = refs
        side_in_ref = refs.pop(-2)
        side_out_ref[...] = side_in_ref[...].astype(BF16)
    if rope:
        h_ref, w_ref, gain_ref, cos_ref, sin_ref, o_ref = refs
    else:
        h_ref, w_ref, gain_ref, o_ref = refs
    h = h_ref[...]
    for jt in range(w_ref.shape[1] // tn):
        y = _dot(h, w_ref[:, jt * tn:(jt + 1) * tn])
        if jt >= n_qk_tiles:
            o_ref[:, jt * tn:(jt + 1) * tn] = y.astype(BF16)
            continue
        gain = gain_ref[0:1, :] * SCORE_SCALE2 if jt < n_q_tiles else gain_ref[1:2, :]
        for hh in range(tn // HEAD_DIM):
            yh = y[:, hh * HEAD_DIM:(hh + 1) * HEAD_DIM]
            r = lax.rsqrt(jnp.mean(yh * yh, axis=-1, keepdims=True) + EPS)
            yh = (yh * r) * gain
            if rope:
                yh = yh * cos_ref[...] + pltpu.roll(yh, HEAD_DIM // 2, 1) * sin_ref[...]
            o_ref[:, jt * tn + hh * HEAD_DIM:jt * tn + (hh + 1) * HEAD_DIM] = yh.astype(BF16)


def _qkv_proj(h, w_qkv_paired, jl, gains_paired, rope_tabs, tm, side_w=None, tn=512):
    m, d = h.shape
    n_out = w_qkv_paired.shape[2]
    d_q = d
    d_kv = (n_out - d_q) // 2
    rope = rope_tabs is not None
    in_specs = [
        pl.BlockSpec((tm, d), lambda i: (i, 0)),
        pl.BlockSpec((None, d, n_out), lambda i: (jl, 0, 0), pipeline_mode=pl.Buffered(1)),
        pl.BlockSpec((2, HEAD_DIM), lambda i: (0, 0)),
    ]
    args = [h, w_qkv_paired, gains_paired]
    if rope:
        in_specs += [pl.BlockSpec((tm, HEAD_DIM), lambda i: (i, 0))] * 2
        args += list(rope_tabs)
    out_specs = [pl.BlockSpec((tm, n_out), lambda i: (i, 0))]
    out_shape = [jax.ShapeDtypeStruct((m, n_out), BF16)]
    if side_w is not None:
        side = _SideCast(side_w, jl, m // tm, lambda i: i)
        in_specs.append(side.in_spec)
        args.append(side.arg)
        out_specs.append(side.out_spec)
        out_shape.append(side.out_shape)
    kern = functools.partial(_qkv_kernel, rope=rope, side=side_w is not None, tn=tn,
                             n_q_tiles=d_q // tn, n_qk_tiles=(d_q + d_kv) // tn)
    outs = pl.pallas_call(
        kern,
        grid=(m // tm,),
        in_specs=in_specs,
        out_specs=out_specs,
        out_shape=out_shape,
        compiler_params=_params("arbitrary"),
        name="qkv_proj",
    )(*args)
    return (outs[0], outs[1]) if side_w is not None else (outs[0], None)


def _sink_softmax(t, sink2):
    rows, n_keys = t.shape
    wide = (rows, HEAD_DIM)
    m2 = jnp.maximum(jnp.broadcast_to(jnp.max(t, axis=-1, keepdims=True), wide), sink2)
    e = jnp.exp2(t - jnp.tile(m2, (1, n_keys // HEAD_DIM)))
    den = jnp.broadcast_to(jnp.sum(e, axis=-1, keepdims=True), wide) + jnp.exp2(sink2 - m2)
    return e.astype(BF16), den


def _scores2(q, k):
    return lax.dot_general(q, k, (((1,), (1,)), ((), ())), preferred_element_type=F32)


def _band_attn_kernel(sink_ref, q_ref, kp_ref, km_ref, kn_ref, kc_ref, vp_ref, vm_ref, vn_ref, vc_ref, o_ref,
                      kw_ref, vw_ref):
    h = pl.program_id(0)
    i = pl.program_id(1)
    tq = q_ref.shape[0]
    nb = tq // WINDOW
    last_blk = pl.num_programs(1) * nb - 1
    kw_ref[0:WINDOW, :] = kp_ref[...]
    kw_ref[WINDOW:WINDOW + tq, :] = km_ref[...]
    kw_ref[WINDOW + tq:, :] = kn_ref[...]
    vw_ref[0:WINDOW, :] = vp_ref[...]
    vw_ref[WINDOW:WINDOW + tq, :] = vm_ref[...]
    vw_ref[WINDOW + tq:, :] = vn_ref[...]
    rows = GROUP * WINDOW
    r = lax.broadcasted_iota(jnp.int32, (rows, WINDOW), 0) & (WINDOW - 1)
    c_minus_r = lax.broadcasted_iota(jnp.int32, (rows, WINDOW), 1) - r
    sink2 = jnp.concatenate([jnp.full((WINDOW, HEAD_DIM), sink_ref[h * GROUP + g] * LOG2E, F32)
                             for g in range(GROUP)], axis=0)
    kc = kc_ref[...]
    vc = vc_ref[...]

    def masked_scores(b):
        blk = i * nb + b
        q4 = jnp.concatenate([q_ref[b * WINDOW:(b + 1) * WINDOW, g * HEAD_DIM:(g + 1) * HEAD_DIM]
                              for g in range(GROUP)], axis=0)
        s = _scores2(q4, jnp.concatenate([kw_ref[b * WINDOW:(b + 3) * WINDOW, :], kc], axis=0))
        lo = jnp.where(blk > 0, 0, WINDOW)
        hi = jnp.where(blk < last_blk, 0, -WINDOW)
        s_prev = jnp.where(c_minus_r >= lo, s[:, 0:WINDOW], NEG_INF)
        s_next = jnp.where(c_minus_r <= hi, s[:, 2 * WINDOW:3 * WINDOW], NEG_INF)
        return jnp.concatenate([s_prev, s[:, WINDOW:2 * WINDOW], s_next, s[:, 3 * WINDOW:]], axis=1)

    def weighted_values(b, e, den):
        o = _dot(e, jnp.concatenate([vw_ref[b * WINDOW:(b + 3) * WINDOW, :], vc], axis=0)) / den
        for g in range(GROUP):
            o_ref[b * WINDOW:(b + 1) * WINDOW, g * HEAD_DIM:(g + 1) * HEAD_DIM] = (
                o[g * WINDOW:(g + 1) * WINDOW, :].astype(BF16))

    for b in range(nb):
        weighted_values(b, *_sink_softmax(masked_scores(b), sink2))


def _band_attention(qkv, qkv_c, sink, tq):
    n = qkv.shape[0]
    l_ctx = qkv_c.shape[0]
    d_q = N_KV_HEADS * GROUP * HEAD_DIM
    kcol = d_q // HEAD_DIM
    vcol = kcol + N_KV_HEADS
    per = tq // WINDOW
    nblk = n // WINDOW

    def prev(col0):
        return pl.BlockSpec((WINDOW, HEAD_DIM), lambda h, i: (jnp.maximum(i * per - 1, 0), col0 + h))

    def main(col0):
        return pl.BlockSpec((tq, HEAD_DIM), lambda h, i: (i, col0 + h))

    def nxt(col0):
        return pl.BlockSpec((WINDOW, HEAD_DIM), lambda h, i: (jnp.minimum((i + 1) * per, nblk - 1), col0 + h))

    def ctx(col0):
        return pl.BlockSpec((l_ctx, HEAD_DIM), lambda h, i: (0, col0 + h))

    return pl.pallas_call(
        _band_attn_kernel,
        grid=(N_KV_HEADS, n // tq),
        in_specs=[
            pl.BlockSpec(memory_space=pltpu.SMEM),
            pl.BlockSpec((tq, GROUP * HEAD_DIM), lambda h, i: (i, h)),
            prev(kcol), main(kcol), nxt(kcol), ctx(kcol),
            prev(vcol), main(vcol), nxt(vcol), ctx(vcol),
        ],
        out_specs=pl.BlockSpec((tq, GROUP * HEAD_DIM), lambda h, i: (i, h)),
        out_shape=jax.ShapeDtypeStruct((n, d_q), BF16),
        scratch_shapes=[pltpu.VMEM((tq + 2 * WINDOW, HEAD_DIM), BF16)] * 2,
        compiler_params=_params("arbitrary", "arbitrary"),
        name="band_attention",
    )(sink, qkv, qkv, qkv, qkv, qkv_c, qkv, qkv, qkv, qkv_c)


def _ctx_attn_kernel(sink_ref, q_ref, k_ref, v_ref, o_ref):
    h = pl.program_id(0)
    for g in range(GROUP):
        sl = slice(g * HEAD_DIM, (g + 1) * HEAD_DIM)
        sink2 = jnp.full((q_ref.shape[0], HEAD_DIM), sink_ref[h * GROUP + g] * LOG2E, F32)
        e, den = _sink_softmax(_scores2(q_ref[:, sl], k_ref[...]), sink2)
        o = _dot(e, v_ref[...]) / den
        o_ref[:, sl] = o.astype(BF16)


def _ctx_attention(qkv_c, sink):
    l_ctx = qkv_c.shape[0]
    d_q = N_KV_HEADS * GROUP * HEAD_DIM
    kcol = d_q // HEAD_DIM
    vcol = kcol + N_KV_HEADS
    return pl.pallas_call(
        _ctx_attn_kernel,
        grid=(N_KV_HEADS,),
        in_specs=[
            pl.BlockSpec(memory_space=pltpu.SMEM),
            pl.BlockSpec((l_ctx, GROUP * HEAD_DIM), lambda h: (0, h)),
            pl.BlockSpec((l_ctx, HEAD_DIM), lambda h: (0, kcol + h)),
            pl.BlockSpec((l_ctx, HEAD_DIM), lambda h: (0, vcol + h)),
        ],
        out_specs=pl.BlockSpec((l_ctx, GROUP * HEAD_DIM), lambda h: (0, h)),
        out_shape=jax.ShapeDtypeStruct((l_ctx, d_q), BF16),
        compiler_params=_params("arbitrary"),
        name="ctx_attention",
    )(sink, qkv_c, qkv_c, qkv_c)


def _conv_in_kernel(h_ref, hp_ref, hn_ref, wb_ref, wc_ref, wv_ref, cw_ref, *rest):
    if len(rest) == 4:
        side_in_ref, u_ref, side_out_ref, hs_ref = rest
        side_out_ref[...] = side_in_ref[...].astype(BF16)
    else:
        u_ref, hs_ref = rest

    @pl.when(pl.program_id(1) == 0)
    def _():
        _fill_halo_lhs(h_ref, hp_ref, hn_ref, hs_ref)

    wb = wb_ref[...].astype(BF16)
    wc = wc_ref[...].astype(BF16)
    wv = wv_ref[...].astype(BF16)
    for c0, rc in _row_chunks(h_ref.shape[0]):
        hs = hs_ref[c0:c0 + rc + 2 * HALO_ROWS, :]
        p = _dot(hs, wc) * _dot(hs, wv)
        gate_b = _dot(hs_ref[c0 + HALO_ROWS:c0 + HALO_ROWS + rc, :], wb)
        u_ref[c0:c0 + rc, :] = (gate_b * _conv3_rows(p, rc, cw_ref)).astype(BF16)


def _conv_in(h, w_in, jl, conv_w, tm, side_w=None, tn=256):
    m, d = h.shape
    nt = d // tn

    def wspec(part):
        return pl.BlockSpec((None, d, tn), lambda i, j: (jl, 0, part * nt + j))

    in_specs = _halo_lhs_specs(tm, d, m) + [wspec(0), wspec(1), wspec(2),
                                            pl.BlockSpec((None, 3, tn), lambda i, j: (jl, 0, j))]
    args = [h, h, h, w_in, w_in, w_in, conv_w]
    out_specs = [pl.BlockSpec((tm, tn), lambda i, j: (i, j))]
    out_shape = [jax.ShapeDtypeStruct((m, d), BF16)]
    if side_w is not None:
        side = _SideCast(side_w, jl, (m // tm) * nt, lambda i, j: i * nt + j)
        in_specs.append(side.in_spec)
        args.append(side.arg)
        out_specs.append(side.out_spec)
        out_shape.append(side.out_shape)
    outs = pl.pallas_call(
        _conv_in_kernel,
        grid=(m // tm, nt),
        in_specs=in_specs,
        out_specs=out_specs,
        out_shape=out_shape,
        scratch_shapes=[pltpu.VMEM((tm + 2 * HALO_ROWS, d), BF16)],
        compiler_params=_params("arbitrary", "arbitrary"),
        name="conv_in",
    )(*args)
    return (outs[0], outs[1]) if side_w is not None else (outs[0], None)


def _ffn_up_kernel(h_ref, hp_ref, hn_ref, wg_ref, wv_ref, cw_ref, cb_ref, *rest):
    if len(rest) == 4:
        side_in_ref, a_ref, side_out_ref, hs_ref = rest
        side_out_ref[...] = side_in_ref[...].astype(BF16)
    else:
        a_ref, hs_ref = rest

    @pl.when(pl.program_id(1) == 0)
    def _():
        _fill_halo_lhs(h_ref, hp_ref, hn_ref, hs_ref)

    wg = wg_ref[...].astype(BF16)
    wv = wv_ref[...].astype(BF16)
    for c0, rc in _row_chunks(h_ref.shape[0]):
        gate = _conv3_rows(_dot(hs_ref[c0:c0 + rc + 2 * HALO_ROWS, :], wg), rc, cw_ref) + cb_ref[...]
        val = _dot(hs_ref[c0 + HALO_ROWS:c0 + HALO_ROWS + rc, :], wv)
        a_ref[c0:c0 + rc, :] = (jax.nn.silu(gate) * val).astype(BF16)


def _ffn_up(h, w_up, layer, conv_w, conv_b, tm, side_w=None, tn=256):
    m, d = h.shape
    f = w_up.shape[2] // 2
    nt = f // tn
    in_specs = _halo_lhs_specs(tm, d, m) + [
        pl.BlockSpec((None, d, tn), lambda i, j: (layer, 0, j)),
        pl.BlockSpec((None, d, tn), lambda i, j: (layer, 0, nt + j)),
        pl.BlockSpec((None, 3, tn), lambda i, j: (layer, 0, j)),
        pl.BlockSpec((None, 1, tn), lambda i, j: (layer, 0, j)),
    ]
    args = [h, h, h, w_up, w_up, conv_w, conv_b.reshape(conv_b.shape[0], 1, f)]
    out_specs = [pl.BlockSpec((tm, tn), lambda i, j: (i, j))]
    out_shape = [jax.ShapeDtypeStruct((m, f), BF16)]
    if side_w is not None:
        side = _SideCast(side_w, layer, (m // tm) * nt, lambda i, j: i * nt + j)
        in_specs.append(side.in_spec)
        args.append(side.arg)
        out_specs.append(side.out_spec)
        out_shape.append(side.out_shape)
    outs = pl.pallas_call(
        _ffn_up_kernel,
        grid=(m // tm, nt),
        in_specs=in_specs,
        out_specs=out_specs,
        out_shape=out_shape,
        scratch_shapes=[pltpu.VMEM((tm + 2 * HALO_ROWS, d), BF16)],
        compiler_params=_params("arbitrary", "arbitrary"),
        name="ffn_up",
    )(*args)
    return (outs[0], outs[1]) if side_w is not None else (outs[0], None)


def _out_proj_kernel(*refs, row, has_next, n_k):
    if has_next:
        a_ref, w_ref, x_ref, g_ref, sh_ref, sc_ref, o_ref, hn_ref = refs
    else:
        a_ref, w_ref, x_ref, g_ref, o_ref = refs
    k = pl.program_id(1)
    last = n_k - 1
    tm = a_ref.shape[0]
    rc = min(OUT_ROW_CHUNK, tm)
    chunks = [slice(c0, c0 + rc) for c0 in range(0, tm, rc)]

    def finish(rows, acc):
        xn = x_ref[rows, :] + g_ref[row:row + 1, :] * acc
        o_ref[rows, :] = xn
        if has_next:
            hn_ref[rows, :] = _modulated(xn, sh_ref, sc_ref, row)

    if n_k == 1:
        for rows in chunks:
            finish(rows, _dot(a_ref[rows, :], w_ref[...]))
        return

    @pl.when(k == 0)
    def _():
        for rows in chunks:
            o_ref[rows, :] = _dot(a_ref[rows, :], w_ref[...])

    @pl.when((k > 0) & (k < last))
    def _():
        for rows in chunks:
            o_ref[rows, :] += _dot(a_ref[rows, :], w_ref[...])

    @pl.when(k == last)
    def _():
        for rows in chunks:
            finish(rows, o_ref[rows, :] + _dot(a_ref[rows, :], w_ref[...]))


def _out_proj(a, w_bf16, jl, x, mod, layer, gate_chunk, nxt, row, tm, tk=512):
    m, d = x.shape
    kdim = a.shape[1]
    in_specs = [
        pl.BlockSpec((tm, tk), lambda i, k: (i, k)),
        pl.BlockSpec((None, tk, d), lambda i, k: (jl, k, 0)),
        pl.BlockSpec((tm, d), lambda i, k: (i, 0)),
        _mod_spec(layer, gate_chunk, d),
    ]
    args = [a, w_bf16, x, mod]
    out_specs = [pl.BlockSpec((tm, d), lambda i, k: (i, 0))]
    out_shape = [jax.ShapeDtypeStruct((m, d), F32)]
    if nxt is not None:
        in_specs += [_mod_spec(nxt[0], nxt[1], d), _mod_spec(nxt[0], nxt[2], d)]
        args += [mod, mod]
        out_specs.append(pl.BlockSpec((tm, d), lambda i, k: (i, 0)))
        out_shape.append(jax.ShapeDtypeStruct((m, d), BF16))
    outs = pl.pallas_call(
        functools.partial(_out_proj_kernel, row=row, has_next=nxt is not None, n_k=kdim // tk),
        grid=(m // tm, kdim // tk),
        in_specs=in_specs,
        out_specs=out_specs,
        out_shape=out_shape,
        compiler_params=_params("arbitrary", "arbitrary"),
        name="out_proj",
    )(*args)
    return (outs[0], outs[1]) if nxt is not None else (outs[0], None)


def _out_proj_resident_kernel(*refs, row, has_next):
    if has_next:
        a_ref, w_ref, x_ref, g_ref, sh_ref, sc_ref, o_ref, hn_ref = refs
    else:
        a_ref, w_ref, x_ref, g_ref, o_ref = refs
    tm = a_ref.shape[0]
    rc = min(OUT_ROW_CHUNK, tm)
    for c0 in range(0, tm, rc):
        rows = slice(c0, c0 + rc)
        xn = x_ref[rows, :] + g_ref[row:row + 1, :] * _dot(a_ref[rows, :], w_ref[...])
        o_ref[rows, :] = xn
        if has_next:
            hn_ref[rows, :] = _modulated(xn, sh_ref, sc_ref, row)


def _out_proj_resident(a, w_bf16, jl, x, mod, layer, gate_chunk, nxt, row, tm):
    m, d = x.shape
    kdim = a.shape[1]
    in_specs = [
        pl.BlockSpec((tm, kdim), lambda i: (i, 0)),
        pl.BlockSpec((None, kdim, d), lambda i: (jl, 0, 0), pipeline_mode=pl.Buffered(1)),
        pl.BlockSpec((tm, d), lambda i: (i, 0)),
        _mod_spec(layer, gate_chunk, d),
    ]
    args = [a, w_bf16, x, mod]
    out_specs = [pl.BlockSpec((tm, d), lambda i: (i, 0))]
    out_shape = [jax.ShapeDtypeStruct((m, d), F32)]
    if nxt is not None:
        in_specs += [_mod_spec(nxt[0], nxt[1], d), _mod_spec(nxt[0], nxt[2], d)]
        args += [mod, mod]
        out_specs.append(pl.BlockSpec((tm, d), lambda i: (i, 0)))
        out_shape.append(jax.ShapeDtypeStruct((m, d), BF16))
    outs = pl.pallas_call(
        functools.partial(_out_proj_resident_kernel, row=row, has_next=nxt is not None),
        grid=(m // tm,),
        in_specs=in_specs,
        out_specs=out_specs,
        out_shape=out_shape,
        compiler_params=_params("arbitrary"),
        name="out_proj_resident",
    )(*args)
    return (outs[0], outs[1]) if nxt is not None else (outs[0], None)


def _rope_tables(n):
    rows = n // GRID_W
    inv = ROPE_BASE ** (-jnp.arange(ROPE_PAIRS, dtype=F32) / ROPE_PAIRS)
    row_ang = jnp.arange(rows, dtype=F32)[:, None] * inv
    col_ang = jnp.arange(GRID_W, dtype=F32)[:, None] * inv

    def per_token(row_tab, col_tab, lo_sign):
        r = jnp.repeat(row_tab, GRID_W, axis=0)
        c = jnp.tile(col_tab, (rows, 1))
        return jnp.concatenate([lo_sign * r, lo_sign * c, r, c], axis=1)

    return per_token(jnp.cos(row_ang), jnp.cos(col_ang), 1.0), per_token(jnp.sin(row_ang), jnp.sin(col_ang), -1.0)


def kernel(x, c, ctx, c_ctx, w_ada, b_ada, attn_w_qkv, attn_w_o, attn_q_gain, attn_k_gain, attn_sink,
           sc_w_in, sc_conv, sc_w_out, ffn_w_up, ffn_conv, ffn_conv_b, ffn_w_down):
    batch, n, d = x.shape
    l_ctx = ctx.shape[1]
    depth = w_ada.shape[0]
    assert batch == 1 and d == N_KV_HEADS * GROUP * HEAD_DIM
    tm_in = min(2048, n)
    tm_out = min(1024, n)
    tm_res = min(512, n)
    tq = min(2048, n)
    tm_c = l_ctx

    cc = jnp.concatenate([c, c_ctx[None, :], jnp.zeros((MOD_ROWS - 2, d), F32)], axis=0)
    mod = _ada_table(cc, w_ada, b_ada)
    rope_tabs = _rope_tables(n)
    n_qk_heads = N_KV_HEADS * GROUP + N_KV_HEADS
    w_qkv = _paired_qkv_weight(attn_w_qkv, n_qk_heads * HEAD_DIM)
    xs, cs = x[0], ctx[0]
    hx = _modulate(xs, mod, 0, ROW_LATENT, min(512, n))
    hc = _modulate(cs, mod, 0, ROW_CTX, tm_c)

    for l in range(depth):
        is_attn = (l % N_MIXERS) == 0
        j = l // N_MIXERS
        need_ctx = l < depth - 1
        nxt_ffn = (l, SHIFT_FFN, SCALE_FFN)
        nxt_mix = (l + 1, SHIFT_MIX, SCALE_MIX) if l + 1 < depth else None
        if is_attn:
            gains = _paired_lanes(jnp.stack([attn_q_gain[j], attn_k_gain[j]]), 1)
            qkv, w_o = _qkv_proj(hx, w_qkv, j, gains, rope_tabs, tm_res, side_w=attn_w_o)
            qkv_c, _ = _qkv_proj(hc, w_qkv, j, gains, None, tm_c)
            o = _band_attention(qkv, qkv_c, attn_sink[j], tq)
            xs, hx = _out_proj_resident(o, w_o[None], 0, xs, mod, l, GATE_MIX, nxt_ffn, ROW_LATENT, tm_res)
            if need_ctx:
                o_c = _ctx_attention(qkv_c, attn_sink[j])
                cs, hc = _out_proj_resident(o_c, w_o[None], 0, cs, mod, l, GATE_MIX, nxt_ffn, ROW_CTX, tm_c)
        else:
            u, w_out = _conv_in(hx, sc_w_in, j, sc_conv, tm_in, side_w=sc_w_out)
            xs, hx = _out_proj_resident(u, w_out[None], 0, xs, mod, l, GATE_MIX, nxt_ffn, ROW_LATENT, tm_res)
            if need_ctx:
                u_c, _ = _conv_in(hc, sc_w_in, j, sc_conv, tm_c)
                cs, hc = _out_proj_resident(u_c, w_out[None], 0, cs, mod, l, GATE_MIX, nxt_ffn, ROW_CTX, tm_c)
        a, w_down = _ffn_up(hx, ffn_w_up, l, ffn_conv, ffn_conv_b, tm_in, side_w=ffn_w_down)
        xs, hx = _out_proj(a, w_down[None], 0, xs, mod, l, GATE_FFN, nxt_mix, ROW_LATENT, tm_out)
        if need_ctx:
            a_c, _ = _ffn_up(hc, ffn_w_up, l, ffn_conv, ffn_conv_b, tm_c)
            cs, hc = _out_proj(a_c, w_down[None], 0, cs, mod, l, GATE_FFN, nxt_mix, ROW_CTX, tm_c)
    return xs[None]
```

```python
import functools

import jax
import jax.numpy as jnp
from jax import lax
from jax.experimental import pallas as pl
from jax.experimental.pallas import tpu as pltpu

HEAD_DIM = 128
N_KV_HEADS = 4
GROUP = 4
WINDOW = 128
GRID_W = 64
ROPE_PAIRS = HEAD_DIM // 4
ROPE_BASE = 10000.0
N_MIXERS = 2
EPS = 1e-6
NEG_INF = -1e30

BF16 = jnp.bfloat16
F32 = jnp.float32

V7X_VMEM_LIMIT_BYTES = 56 * 1024 * 1024
HALO_ROWS = 16
MOD_ROWS = 8
ROW_LATENT = 0
ROW_CTX = 1
ROW_CHUNK = 512
CTX_TILE = 512
OUT_ROW_CHUNK = 256
LOG2E = 1.4426950408889634
SCORE_SCALE2 = HEAD_DIM ** -0.5 * LOG2E
SHIFT_MIX, SCALE_MIX, GATE_MIX, SHIFT_FFN, SCALE_FFN, GATE_FFN = range(6)


def _params(*sem):
    return pltpu.CompilerParams(dimension_semantics=sem, vmem_limit_bytes=V7X_VMEM_LIMIT_BYTES)


def _dot(a, b):
    return jnp.dot(a, b, preferred_element_type=F32)


def _ada_kernel(cc_ref, w_ref, b_ref, o_ref):
    a = jax.nn.silu(cc_ref[...]).astype(BF16)
    o_ref[...] = _dot(a, w_ref[...].astype(BF16)) + b_ref[...]


def _ada_table(cc, w_ada, b_ada, tn=1024):
    depth, d, d6 = w_ada.shape
    return pl.pallas_call(
        _ada_kernel,
        grid=(depth, d6 // tn),
        in_specs=[
            pl.BlockSpec((MOD_ROWS, d), lambda l, j: (0, 0)),
            pl.BlockSpec((None, d, tn), lambda l, j: (l, 0, j)),
            pl.BlockSpec((None, 1, tn), lambda l, j: (l, 0, j)),
        ],
        out_specs=pl.BlockSpec((None, MOD_ROWS, tn), lambda l, j: (l, 0, j)),
        out_shape=jax.ShapeDtypeStruct((depth, MOD_ROWS, d6), F32),
        compiler_params=_params("arbitrary", "arbitrary"),
        name="ada_table",
    )(cc, w_ada, b_ada.reshape(depth, 1, d6))


def _modulated(x, sh_ref, sc_ref, row):
    r = lax.rsqrt(jnp.mean(x * x, axis=-1, keepdims=True) + EPS)
    return ((x * r) * (1 + sc_ref[row:row + 1, :]) + sh_ref[row:row + 1, :]).astype(BF16)


def _mod_spec(layer, chunk, d):
    return pl.BlockSpec((None, MOD_ROWS, d), lambda *_: (layer, 0, chunk))


def _fill_halo_lhs(h_ref, hp_ref, hn_ref, hs_ref):
    i = pl.program_id(0)
    tm = h_ref.shape[0]
    zero = jnp.zeros(hp_ref.shape, BF16)
    hs_ref[0:HALO_ROWS, :] = jnp.where(i > 0, hp_ref[...], zero)
    hs_ref[HALO_ROWS:HALO_ROWS + tm, :] = h_ref[...]
    hs_ref[HALO_ROWS + tm:, :] = jnp.where(i < pl.num_programs(0) - 1, hn_ref[...], zero)


def _halo_lhs_specs(tm, d, m):
    per = tm // HALO_ROWS
    nblk = m // HALO_ROWS
    return [
        pl.BlockSpec((tm, d), lambda i, j: (i, 0)),
        pl.BlockSpec((HALO_ROWS, d), lambda i, j: (jnp.maximum(i * per - 1, 0), 0)),
        pl.BlockSpec((HALO_ROWS, d), lambda i, j: (jnp.minimum((i + 1) * per, nblk - 1), 0)),
    ]


class _SideCast:
    def __init__(self, w, layer, n_steps, step_of):
        _, kdim, d = w.shape
        rows = kdim // n_steps
        assert kdim % n_steps == 0 and rows % HALO_ROWS == 0
        self.arg = w
        self.in_spec = pl.BlockSpec((None, rows, d), lambda *g: (layer, step_of(*g), 0))
        self.out_spec = pl.BlockSpec((rows, d), lambda *g: (step_of(*g), 0))
        self.out_shape = jax.ShapeDtypeStruct((kdim, d), BF16)

    @staticmethod
    def fits(w, n_steps):
        kdim = w.shape[1]
        return kdim % n_steps == 0 and (kdim // n_steps) % HALO_ROWS == 0


def _row_chunks(tm):
    rc = min(ROW_CHUNK, tm)
    return [(c0, rc) for c0 in range(0, tm, rc)]


def _conv3_rows(y, tm, w_ref):
    ext = y.shape[0]
    sl = slice(HALO_ROWS, HALO_ROWS + tm)
    dn = pltpu.roll(y, 1, 0)[sl]
    up = pltpu.roll(y, ext - 1, 0)[sl]
    return dn * w_ref[0:1, :] + y[sl] * w_ref[1:2, :] + up * w_ref[2:3, :]


def _paired_lanes(a, n_heads):
    lead = a.shape[:-1]
    return a.reshape(*lead, n_heads, 2, 2, ROPE_PAIRS).swapaxes(-3, -2).reshape(*lead, n_heads * HEAD_DIM)


def _pair_weight_kernel(w_ref, p_ref, o_ref):
    o_ref[...] = _dot(w_ref[...].astype(BF16), p_ref[...]).astype(BF16)


def _paired_qkv_weight(w_qkv, n_qk_cols, tn=512):
    layers, d, n_out = w_qkv.shape
    eye = jnp.eye(tn, dtype=BF16)
    perms = jnp.stack([_paired_lanes(eye, tn // HEAD_DIM), eye])
    n_qk_tiles = n_qk_cols // tn
    return pl.pallas_call(
        _pair_weight_kernel,
        grid=(layers, n_out // tn),
        in_specs=[
            pl.BlockSpec((None, d, tn), lambda l, j: (l, 0, j)),
            pl.BlockSpec((None, tn, tn), lambda l, j: (jnp.where(j < n_qk_tiles, 0, 1), 0, 0)),
        ],
        out_specs=pl.BlockSpec((None, d, tn), lambda l, j: (l, 0, j)),
        out_shape=jax.ShapeDtypeStruct((layers, d, n_out), BF16),
        compiler_params=_params("arbitrary", "arbitrary"),
        name="pair_qkv_weight",
    )(w_qkv, perms)


def _qkv_kernel(*refs, rope, side, premod_row, tn, n_q_tiles, n_qk_tiles):
    refs = list(refs)
    if side:
        side_out_ref = refs.pop()
        side_in_ref = refs.pop(-2)
        side_out_ref[...] = side_in_ref[...].astype(BF16)
    if premod_row is None:
        h = refs.pop(0)[...]
    else:
        x_ref, sh_ref, sc_ref = refs[:3]
        del refs[:3]
        h = _modulated(x_ref[...], sh_ref, sc_ref, premod_row)
    if rope:
        w_ref, gain_ref, cos_ref, sin_ref, o_ref = refs
    else:
        w_ref, gain_ref, o_ref = refs
    for jt in range(w_ref.shape[1] // tn):
        y = _dot(h, w_ref[:, jt * tn:(jt + 1) * tn])
        if jt >= n_qk_tiles:
            o_ref[:, jt * tn:(jt + 1) * tn] = y.astype(BF16)
            continue
        gain = gain_ref[0:1, :] * SCORE_SCALE2 if jt < n_q_tiles else gain_ref[1:2, :]
        for hh in range(tn // HEAD_DIM):
            yh = y[:, hh * HEAD_DIM:(hh + 1) * HEAD_DIM]
            r = lax.rsqrt(jnp.mean(yh * yh, axis=-1, keepdims=True) + EPS)
            yh = (yh * r) * gain
            if rope:
                yh = yh * cos_ref[...] + pltpu.roll(yh, HEAD_DIM // 2, 1) * sin_ref[...]
            o_ref[:, jt * tn + hh * HEAD_DIM:jt * tn + (hh + 1) * HEAD_DIM] = yh.astype(BF16)


def _qkv_proj(h, w_qkv_paired, jl, gains_paired, rope_tabs, tm, side_w=None, premod=None, tn=512):
    m, d = h.shape
    n_out = w_qkv_paired.shape[2]
    d_q = d
    d_kv = (n_out - d_q) // 2
    rope = rope_tabs is not None
    in_specs = [pl.BlockSpec((tm, d), lambda i: (i, 0))]
    args = [h]
    if premod is not None:
        mod, layer, _ = premod
        in_specs += [_mod_spec(layer, SHIFT_MIX, d), _mod_spec(layer, SCALE_MIX, d)]
        args += [mod, mod]
    in_specs += [
        pl.BlockSpec((None, d, n_out), lambda i: (jl, 0, 0), pipeline_mode=pl.Buffered(1)),
        pl.BlockSpec((2, HEAD_DIM), lambda i: (0, 0)),
    ]
    args += [w_qkv_paired, gains_paired]
    if rope:
        in_specs += [pl.BlockSpec((tm, HEAD_DIM), lambda i: (i, 0))] * 2
        args += list(rope_tabs)
    out_specs = [pl.BlockSpec((tm, n_out), lambda i: (i, 0))]
    out_shape = [jax.ShapeDtypeStruct((m, n_out), BF16)]
    if side_w is not None:
        side = _SideCast(side_w, jl, m // tm, lambda i: i)
        in_specs.append(side.in_spec)
        args.append(side.arg)
        out_specs.append(side.out_spec)
        out_shape.append(side.out_shape)
    kern = functools.partial(_qkv_kernel, rope=rope, side=side_w is not None,
                             premod_row=None if premod is None else premod[2], tn=tn,
                             n_q_tiles=d_q // tn, n_qk_tiles=(d_q + d_kv) // tn)
    outs = pl.pallas_call(
        kern,
        grid=(m // tm,),
        in_specs=in_specs,
        out_specs=out_specs,
        out_shape=out_shape,
        compiler_params=_params("arbitrary"),
        name="qkv_proj",
    )(*args)
    return (outs[0], outs[1]) if side_w is not None else (outs[0], None)


def _sink_attend(t, v, sink2):
    rows, n_keys = t.shape
    m2 = jnp.maximum(jnp.broadcast_to(jnp.max(t, axis=-1, keepdims=True), (rows, HEAD_DIM)), sink2)
    e = jnp.exp2(t - jnp.tile(m2, (1, n_keys // HEAD_DIM))).astype(BF16)
    ov = _dot(e, jnp.concatenate([v, jnp.ones_like(v)], axis=1))
    return ov[:, :HEAD_DIM] / (ov[:, HEAD_DIM:] + jnp.exp2(sink2 - m2))


def _scores2(q, k):
    return lax.dot_general(q, k, (((1,), (1,)), ((), ())), preferred_element_type=F32)


def _band_attn_kernel(sink_ref, q_ref, kp_ref, km_ref, kn_ref, kc_ref, vp_ref, vm_ref, vn_ref, vc_ref, o_ref,
                      kw_ref, vw_ref):
    h = pl.program_id(0)
    i = pl.program_id(1)
    tq = q_ref.shape[0]
    nb = tq // WINDOW
    last_blk = pl.num_programs(1) * nb - 1
    kw_ref[0:WINDOW, :] = kp_ref[...]
    kw_ref[WINDOW:WINDOW + tq, :] = km_ref[...]
    kw_ref[WINDOW + tq:, :] = kn_ref[...]
    vw_ref[0:WINDOW, :] = vp_ref[...]
    vw_ref[WINDOW:WINDOW + tq, :] = vm_ref[...]
    vw_ref[WINDOW + tq:, :] = vn_ref[...]
    rows = GROUP * WINDOW
    r = lax.broadcasted_iota(jnp.int32, (rows, WINDOW), 0) & (WINDOW - 1)
    c_minus_r = lax.broadcasted_iota(jnp.int32, (rows, WINDOW), 1) - r
    sink2 = jnp.concatenate([jnp.full((WINDOW, HEAD_DIM), sink_ref[h * GROUP + g] * LOG2E, F32)
                             for g in range(GROUP)], axis=0)
    kc = kc_ref[...]
    vc = vc_ref[...]

    def masked_scores(b):
        blk = i * nb + b
        q4 = jnp.concatenate([q_ref[b * WINDOW:(b + 1) * WINDOW, g * HEAD_DIM:(g + 1) * HEAD_DIM]
                              for g in range(GROUP)], axis=0)
        s = _scores2(q4, jnp.concatenate([kw_ref[b * WINDOW:(b + 3) * WINDOW, :], kc], axis=0))
        lo = jnp.where(blk > 0, 0, WINDOW)
        hi = jnp.where(blk < last_blk, 0, -WINDOW)
        s_prev = jnp.where(c_minus_r >= lo, s[:, 0:WINDOW], NEG_INF)
        s_next = jnp.where(c_minus_r <= hi, s[:, 2 * WINDOW:3 * WINDOW], NEG_INF)
        return jnp.concatenate([s_prev, s[:, WINDOW:2 * WINDOW], s_next, s[:, 3 * WINDOW:]], axis=1)

    for b in range(nb):
        v = jnp.concatenate([vw_ref[b * WINDOW:(b + 3) * WINDOW, :], vc], axis=0)
        o = _sink_attend(masked_scores(b), v, sink2)
        for g in range(GROUP):
            o_ref[b * WINDOW:(b + 1) * WINDOW, g * HEAD_DIM:(g + 1) * HEAD_DIM] = (
                o[g * WINDOW:(g + 1) * WINDOW, :].astype(BF16))


def _band_attention(qkv, qkv_c, sink, tq):
    n = qkv.shape[0]
    l_ctx = qkv_c.shape[0]
    d_q = N_KV_HEADS * GROUP * HEAD_DIM
    kcol = d_q // HEAD_DIM
    vcol = kcol + N_KV_HEADS
    per = tq // WINDOW
    nblk = n // WINDOW

    def prev(col0):
        return pl.BlockSpec((WINDOW, HEAD_DIM), lambda h, i: (jnp.maximum(i * per - 1, 0), col0 + h))

    def main(col0):
        return pl.BlockSpec((tq, HEAD_DIM), lambda h, i: (i, col0 + h))

    def nxt(col0):
        return pl.BlockSpec((WINDOW, HEAD_DIM), lambda h, i: (jnp.minimum((i + 1) * per, nblk - 1), col0 + h))

    def ctx(col0):
        return pl.BlockSpec((l_ctx, HEAD_DIM), lambda h, i: (0, col0 + h))

    return pl.pallas_call(
        _band_attn_kernel,
        grid=(N_KV_HEADS, n // tq),
        in_specs=[
            pl.BlockSpec(memory_space=pltpu.SMEM),
            pl.BlockSpec((tq, GROUP * HEAD_DIM), lambda h, i: (i, h)),
            prev(kcol), main(kcol), nxt(kcol), ctx(kcol),
            prev(vcol), main(vcol), nxt(vcol), ctx(vcol),
        ],
        out_specs=pl.BlockSpec((tq, GROUP * HEAD_DIM), lambda h, i: (i, h)),
        out_shape=jax.ShapeDtypeStruct((n, d_q), BF16),
        scratch_shapes=[pltpu.VMEM((tq + 2 * WINDOW, HEAD_DIM), BF16)] * 2,
        compiler_params=_params("arbitrary", "arbitrary"),
        name="band_attention",
    )(sink, qkv, qkv, qkv, qkv, qkv_c, qkv, qkv, qkv, qkv_c)


def _ctx_attn_kernel(sink_ref, q_ref, k_ref, v_ref, o_ref):
    h = pl.program_id(0)
    for g in range(GROUP):
        sl = slice(g * HEAD_DIM, (g + 1) * HEAD_DIM)
        sink2 = jnp.full((q_ref.shape[0], HEAD_DIM), sink_ref[h * GROUP + g] * LOG2E, F32)
        o_ref[:, sl] = _sink_attend(_scores2(q_ref[:, sl], k_ref[...]), v_ref[...], sink2).astype(BF16)


def _ctx_attention(qkv_c, sink):
    l_ctx = qkv_c.shape[0]
    d_q = N_KV_HEADS * GROUP * HEAD_DIM
    kcol = d_q // HEAD_DIM
    vcol = kcol + N_KV_HEADS
    return pl.pallas_call(
        _ctx_attn_kernel,
        grid=(N_KV_HEADS,),
        in_specs=[
            pl.BlockSpec(memory_space=pltpu.SMEM),
            pl.BlockSpec((l_ctx, GROUP * HEAD_DIM), lambda h: (0, h)),
            pl.BlockSpec((l_ctx, HEAD_DIM), lambda h: (0, kcol + h)),
            pl.BlockSpec((l_ctx, HEAD_DIM), lambda h: (0, vcol + h)),
        ],
        out_specs=pl.BlockSpec((l_ctx, GROUP * HEAD_DIM), lambda h: (0, h)),
        out_shape=jax.ShapeDtypeStruct((l_ctx, d_q), BF16),
        compiler_params=_params("arbitrary"),
        name="ctx_attention",
    )(sink, qkv_c, qkv_c, qkv_c)


def _conv_in_kernel(h_ref, hp_ref, hn_ref, wb_ref, wc_ref, wv_ref, cw_ref, *rest):
    if len(rest) == 4:
        side_in_ref, u_ref, side_out_ref, hs_ref = rest
        side_out_ref[...] = side_in_ref[...].astype(BF16)
    else:
        u_ref, hs_ref = rest

    @pl.when(pl.program_id(1) == 0)
    def _():
        _fill_halo_lhs(h_ref, hp_ref, hn_ref, hs_ref)

    wb = wb_ref[...].astype(BF16)
    wc = wc_ref[...].astype(BF16)
    wv = wv_ref[...].astype(BF16)
    for c0, rc in _row_chunks(h_ref.shape[0]):
        hs = hs_ref[c0:c0 + rc + 2 * HALO_ROWS, :]
        p = _dot(hs, wc) * _dot(hs, wv)
        gate_b = _dot(hs_ref[c0 + HALO_ROWS:c0 + HALO_ROWS + rc, :], wb)
        u_ref[c0:c0 + rc, :] = (gate_b * _conv3_rows(p, rc, cw_ref)).astype(BF16)


def _conv_in(h, w_in, jl, conv_w, tm, side_w=None, tn=256):
    m, d = h.shape
    nt = d // tn

    def wspec(part):
        return pl.BlockSpec((None, d, tn), lambda i, j: (jl, 0, part * nt + j))

    in_specs = _halo_lhs_specs(tm, d, m) + [wspec(0), wspec(1), wspec(2),
                                            pl.BlockSpec((None, 3, tn), lambda i, j: (jl, 0, j))]
    args = [h, h, h, w_in, w_in, w_in, conv_w]
    out_specs = [pl.BlockSpec((tm, tn), lambda i, j: (i, j))]
    out_shape = [jax.ShapeDtypeStruct((m, d), BF16)]
    if side_w is not None:
        side = _SideCast(side_w, jl, (m // tm) * nt, lambda i, j: i * nt + j)
        in_specs.append(side.in_spec)
        args.append(side.arg)
        out_specs.append(side.out_spec)
        out_shape.append(side.out_shape)
    outs = pl.pallas_call(
        _conv_in_kernel,
        grid=(m // tm, nt),
        in_specs=in_specs,
        out_specs=out_specs,
        out_shape=out_shape,
        scratch_shapes=[pltpu.VMEM((tm + 2 * HALO_ROWS, d), BF16)],
        compiler_params=_params("arbitrary", "arbitrary"),
        name="conv_in",
    )(*args)
    return (outs[0], outs[1]) if side_w is not None else (outs[0], None)


def _ffn_up_kernel(h_ref, hp_ref, hn_ref, wg_ref, wv_ref, cw_ref, cb_ref, *rest):
    if len(rest) == 4:
        side_in_ref, a_ref, side_out_ref, hs_ref = rest
        side_out_ref[...] = side_in_ref[...].astype(BF16)
    else:
        a_ref, hs_ref = rest

    @pl.when(pl.program_id(1) == 0)
    def _():
        _fill_halo_lhs(h_ref, hp_ref, hn_ref, hs_ref)

    wg = wg_ref[...].astype(BF16)
    wv = wv_ref[...].astype(BF16)
    for c0, rc in _row_chunks(h_ref.shape[0]):
        gate = _conv3_rows(_dot(hs_ref[c0:c0 + rc + 2 * HALO_ROWS, :], wg), rc, cw_ref) + cb_ref[...]
        val = _dot(hs_ref[c0 + HALO_ROWS:c0 + HALO_ROWS + rc, :], wv)
        a_ref[c0:c0 + rc, :] = (jax.nn.silu(gate) * val).astype(BF16)


def _ffn_up(h, w_up, layer, conv_w, conv_b, tm, side_w=None, tn=256):
    m, d = h.shape
    f = w_up.shape[2] // 2
    nt = f // tn
    in_specs = _halo_lhs_specs(tm, d, m) + [
        pl.BlockSpec((None, d, tn), lambda i, j: (layer, 0, j)),
        pl.BlockSpec((None, d, tn), lambda i, j: (layer, 0, nt + j)),
        pl.BlockSpec((None, 3, tn), lambda i, j: (layer, 0, j)),
        pl.BlockSpec((None, 1, tn), lambda i, j: (layer, 0, j)),
    ]
    args = [h, h, h, w_up, w_up, conv_w, conv_b.reshape(conv_b.shape[0], 1, f)]
    out_specs = [pl.BlockSpec((tm, tn), lambda i, j: (i, j))]
    out_shape = [jax.ShapeDtypeStruct((m, f), BF16)]
    if side_w is not None:
        side = _SideCast(side_w, layer, (m // tm) * nt, lambda i, j: i * nt + j)
        in_specs.append(side.in_spec)
        args.append(side.arg)
        out_specs.append(side.out_spec)
        out_shape.append(side.out_shape)
    outs = pl.pallas_call(
        _ffn_up_kernel,
        grid=(m // tm, nt),
        in_specs=in_specs,
        out_specs=out_specs,
        out_shape=out_shape,
        scratch_shapes=[pltpu.VMEM((tm + 2 * HALO_ROWS, d), BF16)],
        compiler_params=_params("arbitrary", "arbitrary"),
        name="ffn_up",
    )(*args)
    return (outs[0], outs[1]) if side_w is not None else (outs[0], None)


def _out_proj_kernel(*refs, row, has_next, n_k):
    if has_next:
        a_ref, w_ref, x_ref, g_ref, sh_ref, sc_ref, o_ref, hn_ref = refs
    else:
        a_ref, w_ref, x_ref, g_ref, o_ref = refs
    k = pl.program_id(1)
    last = n_k - 1
    tm = a_ref.shape[0]
    rc = min(OUT_ROW_CHUNK, tm)
    chunks = [slice(c0, c0 + rc) for c0 in range(0, tm, rc)]

    def finish(rows, acc):
        xn = x_ref[rows, :] + g_ref[row:row + 1, :] * acc
        o_ref[rows, :] = xn
        if has_next:
            hn_ref[rows, :] = _modulated(xn, sh_ref, sc_ref, row)

    if n_k == 1:
        for rows in chunks:
            finish(rows, _dot(a_ref[rows, :], w_ref[...]))
        return

    @pl.when(k == 0)
    def _():
        for rows in chunks:
            o_ref[rows, :] = _dot(a_ref[rows, :], w_ref[...])

    @pl.when((k > 0) & (k < last))
    def _():
        for rows in chunks:
            o_ref[rows, :] += _dot(a_ref[rows, :], w_ref[...])

    @pl.when(k == last)
    def _():
        for rows in chunks:
            finish(rows, o_ref[rows, :] + _dot(a_ref[rows, :], w_ref[...]))


def _out_proj(a, w_bf16, jl, x, mod, layer, gate_chunk, nxt, row, tm, tk=512):
    m, d = x.shape
    kdim = a.shape[1]
    in_specs = [
        pl.BlockSpec((tm, tk), lambda i, k: (i, k)),
        pl.BlockSpec((None, tk, d), lambda i, k: (jl, k, 0)),
        pl.BlockSpec((tm, d), lambda i, k: (i, 0)),
        _mod_spec(layer, gate_chunk, d),
    ]
    args = [a, w_bf16, x, mod]
    out_specs = [pl.BlockSpec((tm, d), lambda i, k: (i, 0))]
    out_shape = [jax.ShapeDtypeStruct((m, d), F32)]
    if nxt is not None:
        in_specs += [_mod_spec(nxt[0], nxt[1], d), _mod_spec(nxt[0], nxt[2], d)]
        args += [mod, mod]
        out_specs.append(pl.BlockSpec((tm, d), lambda i, k: (i, 0)))
        out_shape.append(jax.ShapeDtypeStruct((m, d), BF16))
    outs = pl.pallas_call(
        functools.partial(_out_proj_kernel, row=row, has_next=nxt is not None, n_k=kdim // tk),
        grid=(m // tm, kdim // tk),
        in_specs=in_specs,
        out_specs=out_specs,
        out_shape=out_shape,
        compiler_params=_params("arbitrary", "arbitrary"),
        name="out_proj",
    )(*args)
    return (outs[0], outs[1]) if nxt is not None else (outs[0], None)


def _out_proj_resident_kernel(*refs, row, has_next):
    if has_next:
        a_ref, w_ref, x_ref, g_ref, sh_ref, sc_ref, o_ref, hn_ref = refs
    else:
        a_ref, w_ref, x_ref, g_ref, o_ref = refs
    tm = a_ref.shape[0]
    rc = min(OUT_ROW_CHUNK, tm)
    for c0 in range(0, tm, rc):
        rows = slice(c0, c0 + rc)
        xn = x_ref[rows, :] + g_ref[row:row + 1, :] * _dot(a_ref[rows, :], w_ref[...])
        o_ref[rows, :] = xn
        if has_next:
            hn_ref[rows, :] = _modulated(xn, sh_ref, sc_ref, row)


def _out_proj_resident(a, w_bf16, jl, x, mod, layer, gate_chunk, nxt, row, tm):
    m, d = x.shape
    kdim = a.shape[1]
    in_specs = [
        pl.BlockSpec((tm, kdim), lambda i: (i, 0)),
        pl.BlockSpec((None, kdim, d), lambda i: (jl, 0, 0), pipeline_mode=pl.Buffered(1)),
        pl.BlockSpec((tm, d), lambda i: (i, 0)),
        _mod_spec(layer, gate_chunk, d),
    ]
    args = [a, w_bf16, x, mod]
    out_specs = [pl.BlockSpec((tm, d), lambda i: (i, 0))]
    out_shape = [jax.ShapeDtypeStruct((m, d), F32)]
    if nxt is not None:
        in_specs += [_mod_spec(nxt[0], nxt[1], d), _mod_spec(nxt[0], nxt[2], d)]
        args += [mod, mod]
        out_specs.append(pl.BlockSpec((tm, d), lambda i: (i, 0)))
        out_shape.append(jax.ShapeDtypeStruct((m, d), BF16))
    outs = pl.pallas_call(
        functools.partial(_out_proj_resident_kernel, row=row, has_next=nxt is not None),
        grid=(m // tm,),
        in_specs=in_specs,
        out_specs=out_specs,
        out_shape=out_shape,
        compiler_params=_params("arbitrary"),
        name="out_proj_resident",
    )(*args)
    return (outs[0], outs[1]) if nxt is not None else (outs[0], None)


def _rope_tables(n):
    rows = n // GRID_W
    inv = ROPE_BASE ** (-jnp.arange(ROPE_PAIRS, dtype=F32) / ROPE_PAIRS)
    row_ang = jnp.arange(rows, dtype=F32)[:, None] * inv
    col_ang = jnp.arange(GRID_W, dtype=F32)[:, None] * inv

    def per_token(row_tab, col_tab, lo_sign):
        r = jnp.repeat(row_tab, GRID_W, axis=0)
        c = jnp.tile(col_tab, (rows, 1))
        return jnp.concatenate([lo_sign * r, lo_sign * c, r, c], axis=1)

    return per_token(jnp.cos(row_ang), jnp.cos(col_ang), 1.0), per_token(jnp.sin(row_ang), jnp.sin(col_ang), -1.0)


def kernel(x, c, ctx, c_ctx, w_ada, b_ada, attn_w_qkv, attn_w_o, attn_q_gain, attn_k_gain, attn_sink,
           sc_w_in, sc_conv, sc_w_out, ffn_w_up, ffn_conv, ffn_conv_b, ffn_w_down):
    batch, n, d = x.shape
    l_ctx = ctx.shape[1]
    depth = w_ada.shape[0]
    assert batch == 1 and d == N_KV_HEADS * GROUP * HEAD_DIM
    tm_in = min(2048, n)
    tm_out = min(1024, n)
    tm_res = min(512, n)
    tq = min(2048, n)
    tm_c = l_ctx

    cc = jnp.concatenate([c, c_ctx[None, :], jnp.zeros((MOD_ROWS - 2, d), F32)], axis=0)
    mod = _ada_table(cc, w_ada, b_ada)
    rope_tabs = _rope_tables(n)
    n_qk_heads = N_KV_HEADS * GROUP + N_KV_HEADS
    w_qkv = _paired_qkv_weight(attn_w_qkv, n_qk_heads * HEAD_DIM)
    xs, cs = x[0], ctx[0]
    hx = hc = None

    for l in range(depth):
        is_attn = (l % N_MIXERS) == 0
        j = l // N_MIXERS
        need_ctx = l < depth - 1
        nxt_ffn = (l, SHIFT_FFN, SCALE_FFN)
        nxt_mix = (l + 1, SHIFT_MIX, SCALE_MIX) if l + 1 < depth else None
        if is_attn:
            gains = _paired_lanes(jnp.stack([attn_q_gain[j], attn_k_gain[j]]), 1)
            if l == 0:
                qkv, w_o = _qkv_proj(xs, w_qkv, j, gains, rope_tabs, tm_res, side_w=attn_w_o,
                                     premod=(mod, l, ROW_LATENT))
                qkv_c, _ = _qkv_proj(cs, w_qkv, j, gains, None, tm_c, premod=(mod, l, ROW_CTX))
            else:
                qkv, w_o = _qkv_proj(hx, w_qkv, j, gains, rope_tabs, tm_res, side_w=attn_w_o)
                qkv_c, _ = _qkv_proj(hc, w_qkv, j, gains, None, tm_c)
            o = _band_attention(qkv, qkv_c, attn_sink[j], tq)
            xs, hx = _out_proj_resident(o, w_o[None], 0, xs, mod, l, GATE_MIX, nxt_ffn, ROW_LATENT, tm_res)
            if need_ctx:
                o_c = _ctx_attention(qkv_c, attn_sink[j])
                cs, hc = _out_proj_resident(o_c, w_o[None], 0, cs, mod, l, GATE_MIX, nxt_ffn, ROW_CTX, tm_c)
        else:
            u, w_out = _conv_in(hx, sc_w_in, j, sc_conv, tm_in, side_w=sc_w_out)
            xs, hx = _out_proj_resident(u, w_out[None], 0, xs, mod, l, GATE_MIX, nxt_ffn, ROW_LATENT, tm_res)
            if need_ctx:
                u_c, _ = _conv_in(hc, sc_w_in, j, sc_conv, tm_c, tn=CTX_TILE)
                cs, hc = _out_proj_resident(u_c, w_out[None], 0, cs, mod, l, GATE_MIX, nxt_ffn, ROW_CTX, tm_c)
        a, w_down = _ffn_up(hx, ffn_w_up, l, ffn_conv, ffn_conv_b, tm_in, side_w=ffn_w_down)
        xs, hx = _out_proj(a, w_down[None], 0, xs, mod, l, GATE_FFN, nxt_mix, ROW_LATENT, tm_out)
        if need_ctx:
            a_c, _ = _ffn_up(hc, ffn_w_up, l, ffn_conv, ffn_conv_b, tm_c, tn=CTX_TILE)
            f_half = a_c.shape[1] // 2
            tk_c = f_half if f_half % HEAD_DIM == 0 else CTX_TILE
            cs, hc = _out_proj(a_c, w_down[None], 0, cs, mod, l, GATE_FFN, nxt_mix, ROW_CTX, tm_c, tk=tk_c)
    return xs[None]
```

```python
import functools

import jax
import jax.numpy as jnp
from jax import lax
from jax.experimental import pallas as pl
from jax.experimental.pallas import tpu as pltpu

HEAD_DIM = 128
N_KV_HEADS = 4
GROUP = 4
WINDOW = 128
GRID_W = 64
ROPE_PAIRS = HEAD_DIM // 4
ROPE_BASE = 10000.0
N_MIXERS = 2
EPS = 1e-6
NEG_INF = -1e30

BF16 = jnp.bfloat16
F32 = jnp.float32

V7X_VMEM_LIMIT_BYTES = 56 * 1024 * 1024
HALO_ROWS = 16
F32_ROWS = 8
MOD_ROWS = 8
ROW_LATENT = 0
ROW_CTX = 1
ROW_CHUNK = 512
CTX_TILE = 512
OUT_ROW_CHUNK = 256
LOG2E = 1.4426950408889634
SCORE_SCALE2 = HEAD_DIM ** -0.5 * LOG2E
SHIFT_MIX, SCALE_MIX, GATE_MIX, SHIFT_FFN, SCALE_FFN, GATE_FFN = range(6)


def _params(*sem):
    return pltpu.CompilerParams(dimension_semantics=sem, vmem_limit_bytes=V7X_VMEM_LIMIT_BYTES)


def _dot(a, b):
    return jnp.dot(a, b, preferred_element_type=F32)


def _ada_kernel(cc_ref, w_ref, b_ref, o_ref):
    a = jax.nn.silu(cc_ref[...]).astype(BF16)
    o_ref[...] = _dot(a, w_ref[...].astype(BF16)) + b_ref[...]


def _ada_table(cc, w_ada, b_ada, tn=1024):
    depth, d, d6 = w_ada.shape
    return pl.pallas_call(
        _ada_kernel,
        grid=(depth, d6 // tn),
        in_specs=[
            pl.BlockSpec((MOD_ROWS, d), lambda l, j: (0, 0)),
            pl.BlockSpec((None, d, tn), lambda l, j: (l, 0, j)),
            pl.BlockSpec((None, 1, tn), lambda l, j: (l, 0, j)),
        ],
        out_specs=pl.BlockSpec((None, MOD_ROWS, tn), lambda l, j: (l, 0, j)),
        out_shape=jax.ShapeDtypeStruct((depth, MOD_ROWS, d6), F32),
        compiler_params=_params("arbitrary", "arbitrary"),
        name="ada_table",
    )(cc, w_ada, b_ada.reshape(depth, 1, d6))


def _modulated(x, sh_ref, sc_ref, row):
    r = lax.rsqrt(jnp.mean(x * x, axis=-1, keepdims=True) + EPS)
    return ((x * r) * (1 + sc_ref[row:row + 1, :]) + sh_ref[row:row + 1, :]).astype(BF16)


def _mod_spec(layer, chunk, d):
    return pl.BlockSpec((None, MOD_ROWS, d), lambda *_: (layer, 0, chunk))


def _fill_halo_lhs(h_ref, hp_ref, hn_ref, hs_ref):
    i = pl.program_id(0)
    tm = h_ref.shape[0]
    zero = jnp.zeros(hp_ref.shape, BF16)
    hs_ref[0:HALO_ROWS, :] = jnp.where(i > 0, hp_ref[...], zero)
    hs_ref[HALO_ROWS:HALO_ROWS + tm, :] = h_ref[...]
    hs_ref[HALO_ROWS + tm:, :] = jnp.where(i < pl.num_programs(0) - 1, hn_ref[...], zero)


def _halo_lhs_specs(tm, d, m):
    per = tm // HALO_ROWS
    nblk = m // HALO_ROWS
    return [
        pl.BlockSpec((tm, d), lambda i, j: (i, 0)),
        pl.BlockSpec((HALO_ROWS, d), lambda i, j: (jnp.maximum(i * per - 1, 0), 0)),
        pl.BlockSpec((HALO_ROWS, d), lambda i, j: (jnp.minimum((i + 1) * per, nblk - 1), 0)),
    ]


class _SideCast:
    def __init__(self, w, layer, n_steps, step_of):
        _, kdim, d = w.shape
        rows = kdim // n_steps
        assert kdim % n_steps == 0 and rows % HALO_ROWS == 0
        self.arg = w
        self.in_spec = pl.BlockSpec((None, rows, d), lambda *g: (layer, step_of(*g), 0))
        self.out_spec = pl.BlockSpec((rows, d), lambda *g: (step_of(*g), 0))
        self.out_shape = jax.ShapeDtypeStruct((kdim, d), BF16)

    @staticmethod
    def fits(w, n_steps):
        kdim = w.shape[1]
        return kdim % n_steps == 0 and (kdim // n_steps) % HALO_ROWS == 0


def _row_chunks(tm):
    rc = min(ROW_CHUNK, tm)
    return [(c0, rc) for c0 in range(0, tm, rc)]


def _chunked_conv3(hs_ref, tm, project, w_ref, finish):
    chunks = _row_chunks(tm)
    last = len(chunks) - 1
    ys = []

    def body(ci):
        off = HALO_ROWS if ci == 0 else 0
        return ys[ci][off:off + chunks[ci][1]]

    def emit(ci):
        c0, rc = chunks[ci]
        before = ys[0][HALO_ROWS - F32_ROWS:HALO_ROWS] if ci == 0 else body(ci - 1)[-F32_ROWS:]
        if ci == last:
            off = HALO_ROWS if ci == 0 else 0
            after = ys[ci][off + rc:off + rc + F32_ROWS]
        else:
            after = body(ci + 1)[:F32_ROWS]
        ext = jnp.concatenate([before, body(ci), after], axis=0)
        sl = slice(F32_ROWS, F32_ROWS + rc)
        dn = pltpu.roll(ext, 1, 0)[sl]
        up = pltpu.roll(ext, ext.shape[0] - 1, 0)[sl]
        finish(c0, rc, dn * w_ref[0:1, :] + ext[sl] * w_ref[1:2, :] + up * w_ref[2:3, :])

    for ci, (c0, rc) in enumerate(chunks):
        lo = c0 + (0 if ci == 0 else HALO_ROWS)
        hi = c0 + rc + HALO_ROWS + (HALO_ROWS if ci == last else 0)
        ys.append(project(hs_ref[lo:hi, :]))
        if ci >= 1:
            emit(ci - 1)
    emit(last)


def _paired_lanes(a, n_heads):
    lead = a.shape[:-1]
    return a.reshape(*lead, n_heads, 2, 2, ROPE_PAIRS).swapaxes(-3, -2).reshape(*lead, n_heads * HEAD_DIM)


def _pair_weight_kernel(w_ref, p_ref, o_ref):
    o_ref[...] = _dot(w_ref[...].astype(BF16), p_ref[...]).astype(BF16)


def _paired_qkv_weight(w_qkv, n_qk_cols, tn=512):
    layers, d, n_out = w_qkv.shape
    eye = jnp.eye(tn, dtype=BF16)
    perms = jnp.stack([_paired_lanes(eye, tn // HEAD_DIM), eye])
    n_qk_tiles = n_qk_cols // tn
    return pl.pallas_call(
        _pair_weight_kernel,
        grid=(layers, n_out // tn),
        in_specs=[
            pl.BlockSpec((None, d, tn), lambda l, j: (l, 0, j)),
            pl.BlockSpec((None, tn, tn), lambda l, j: (jnp.where(j < n_qk_tiles, 0, 1), 0, 0)),
        ],
        out_specs=pl.BlockSpec((None, d, tn), lambda l, j: (l, 0, j)),
        out_shape=jax.ShapeDtypeStruct((layers, d, n_out), BF16),
        compiler_params=_params("arbitrary", "arbitrary"),
        name="pair_qkv_weight",
    )(w_qkv, perms)


def _qkv_kernel(*refs, rope, side, premod_row, tn, n_q_tiles, n_qk_tiles):
    refs = list(refs)
    if side:
        side_out_ref = refs.pop()
        side_in_ref = refs.pop(-2)
        side_out_ref[...] = side_in_ref[...].astype(BF16)
    if premod_row is None:
        h = refs.pop(0)[...]
    else:
        x_ref, sh_ref, sc_ref = refs[:3]
        del refs[:3]
        h = _modulated(x_ref[...], sh_ref, sc_ref, premod_row)
    if rope:
        w_ref, gain_ref, cos_ref, sin_ref, o_ref = refs
    else:
        w_ref, gain_ref, o_ref = refs
    for jt in range(w_ref.shape[1] // tn):
        y = _dot(h, w_ref[:, jt * tn:(jt + 1) * tn])
        if jt >= n_qk_tiles:
            o_ref[:, jt * tn:(jt + 1) * tn] = y.astype(BF16)
            continue
        gain = gain_ref[0:1, :] * SCORE_SCALE2 if jt < n_q_tiles else gain_ref[1:2, :]
        for hh in range(tn // HEAD_DIM):
            yh = y[:, hh * HEAD_DIM:(hh + 1) * HEAD_DIM]
            r = lax.rsqrt(jnp.mean(yh * yh, axis=-1, keepdims=True) + EPS)
            yh = (yh * r) * gain
            if rope:
                yh = yh * cos_ref[...] + pltpu.roll(yh, HEAD_DIM // 2, 1) * sin_ref[...]
            o_ref[:, jt * tn + hh * HEAD_DIM:jt * tn + (hh + 1) * HEAD_DIM] = yh.astype(BF16)


def _qkv_proj(h, w_qkv_paired, jl, gains_paired, rope_tabs, tm, side_w=None, premod=None, tn=512):
    m, d = h.shape
    n_out = w_qkv_paired.shape[2]
    d_q = d
    d_kv = (n_out - d_q) // 2
    rope = rope_tabs is not None
    in_specs = [pl.BlockSpec((tm, d), lambda i: (i, 0))]
    args = [h]
    if premod is not None:
        mod, layer, _ = premod
        in_specs += [_mod_spec(layer, SHIFT_MIX, d), _mod_spec(layer, SCALE_MIX, d)]
        args += [mod, mod]
    in_specs += [
        pl.BlockSpec((None, d, n_out), lambda i: (jl, 0, 0), pipeline_mode=pl.Buffered(1)),
        pl.BlockSpec((2, HEAD_DIM), lambda i: (0, 0)),
    ]
    args += [w_qkv_paired, gains_paired]
    if rope:
        in_specs += [pl.BlockSpec((tm, HEAD_DIM), lambda i: (i, 0))] * 2
        args += list(rope_tabs)
    out_specs = [pl.BlockSpec((tm, n_out), lambda i: (i, 0))]
    out_shape = [jax.ShapeDtypeStruct((m, n_out), BF16)]
    if side_w is not None:
        side = _SideCast(side_w, jl, m // tm, lambda i: i)
        in_specs.append(side.in_spec)
        args.append(side.arg)
        out_specs.append(side.out_spec)
        out_shape.append(side.out_shape)
    kern = functools.partial(_qkv_kernel, rope=rope, side=side_w is not None,
                             premod_row=None if premod is None else premod[2], tn=tn,
                             n_q_tiles=d_q // tn, n_qk_tiles=(d_q + d_kv) // tn)
    outs = pl.pallas_call(
        kern,
        grid=(m // tm,),
        in_specs=in_specs,
        out_specs=out_specs,
        out_shape=out_shape,
        compiler_params=_params("arbitrary"),
        name="qkv_proj",
    )(*args)
    return (outs[0], outs[1]) if side_w is not None else (outs[0], None)


def _sink_attend(t, v, sink2):
    rows, n_keys = t.shape
    m2 = jnp.maximum(jnp.broadcast_to(jnp.max(t, axis=-1, keepdims=True), (rows, HEAD_DIM)), sink2)
    e = jnp.exp2(t - jnp.tile(m2, (1, n_keys // HEAD_DIM))).astype(BF16)
    ov = _dot(e, jnp.concatenate([v, jnp.ones_like(v)], axis=1))
    return ov[:, :HEAD_DIM] / (ov[:, HEAD_DIM:] + jnp.exp2(sink2 - m2))


def _scores2(q, k):
    return lax.dot_general(q, k, (((1,), (1,)), ((), ())), preferred_element_type=F32)


def _band_attn_kernel(sink_ref, q_ref, kp_ref, km_ref, kn_ref, kc_ref, vp_ref, vm_ref, vn_ref, vc_ref, o_ref,
                      kw_ref, vw_ref):
    h = pl.program_id(0)
    i = pl.program_id(1)
    tq = q_ref.shape[0]
    nb = tq // WINDOW
    last_blk = pl.num_programs(1) * nb - 1
    kw_ref[0:WINDOW, :] = kp_ref[...]
    kw_ref[WINDOW:WINDOW + tq, :] = km_ref[...]
    kw_ref[WINDOW + tq:, :] = kn_ref[...]
    vw_ref[0:WINDOW, :] = vp_ref[...]
    vw_ref[WINDOW:WINDOW + tq, :] = vm_ref[...]
    vw_ref[WINDOW + tq:, :] = vn_ref[...]
    rows = GROUP * WINDOW
    r = lax.broadcasted_iota(jnp.int32, (rows, WINDOW), 0) & (WINDOW - 1)
    c_minus_r = lax.broadcasted_iota(jnp.int32, (rows, WINDOW), 1) - r
    sink2 = jnp.concatenate([jnp.full((WINDOW, HEAD_DIM), sink_ref[h * GROUP + g] * LOG2E, F32)
                             for g in range(GROUP)], axis=0)
    kc = kc_ref[...]
    vc = vc_ref[...]

    def masked_scores(b):
        blk = i * nb + b
        q4 = jnp.concatenate([q_ref[b * WINDOW:(b + 1) * WINDOW, g * HEAD_DIM:(g + 1) * HEAD_DIM]
                              for g in range(GROUP)], axis=0)
        s = _scores2(q4, jnp.concatenate([kw_ref[b * WINDOW:(b + 3) * WINDOW, :], kc], axis=0))
        lo = jnp.where(blk > 0, 0, WINDOW)
        hi = jnp.where(blk < last_blk, 0, -WINDOW)
        s_prev = jnp.where(c_minus_r >= lo, s[:, 0:WINDOW], NEG_INF)
        s_next = jnp.where(c_minus_r <= hi, s[:, 2 * WINDOW:3 * WINDOW], NEG_INF)
        return jnp.concatenate([s_prev, s[:, WINDOW:2 * WINDOW], s_next, s[:, 3 * WINDOW:]], axis=1)

    for b in range(nb):
        v = jnp.concatenate([vw_ref[b * WINDOW:(b + 3) * WINDOW, :], vc], axis=0)
        o = _sink_attend(masked_scores(b), v, sink2)
        for g in range(GROUP):
            o_ref[b * WINDOW:(b + 1) * WINDOW, g * HEAD_DIM:(g + 1) * HEAD_DIM] = (
                o[g * WINDOW:(g + 1) * WINDOW, :].astype(BF16))


def _band_attention(qkv, qkv_c, sink, tq):
    n = qkv.shape[0]
    l_ctx = qkv_c.shape[0]
    d_q = N_KV_HEADS * GROUP * HEAD_DIM
    kcol = d_q // HEAD_DIM
    vcol = kcol + N_KV_HEADS
    per = tq // WINDOW
    nblk = n // WINDOW

    def prev(col0):
        return pl.BlockSpec((WINDOW, HEAD_DIM), lambda h, i: (jnp.maximum(i * per - 1, 0), col0 + h))

    def main(col0):
        return pl.BlockSpec((tq, HEAD_DIM), lambda h, i: (i, col0 + h))

    def nxt(col0):
        return pl.BlockSpec((WINDOW, HEAD_DIM), lambda h, i: (jnp.minimum((i + 1) * per, nblk - 1), col0 + h))

    def ctx(col0):
        return pl.BlockSpec((l_ctx, HEAD_DIM), lambda h, i: (0, col0 + h))

    return pl.pallas_call(
        _band_attn_kernel,
        grid=(N_KV_HEADS, n // tq),
        in_specs=[
            pl.BlockSpec(memory_space=pltpu.SMEM),
            pl.BlockSpec((tq, GROUP * HEAD_DIM), lambda h, i: (i, h)),
            prev(kcol), main(kcol), nxt(kcol), ctx(kcol),
            prev(vcol), main(vcol), nxt(vcol), ctx(vcol),
        ],
        out_specs=pl.BlockSpec((tq, GROUP * HEAD_DIM), lambda h, i: (i, h)),
        out_shape=jax.ShapeDtypeStruct((n, d_q), BF16),
        scratch_shapes=[pltpu.VMEM((tq + 2 * WINDOW, HEAD_DIM), BF16)] * 2,
        compiler_params=_params("arbitrary", "arbitrary"),
        name="band_attention",
    )(sink, qkv, qkv, qkv, qkv, qkv_c, qkv, qkv, qkv, qkv_c)


def _ctx_attn_kernel(sink_ref, q_ref, k_ref, v_ref, o_ref):
    h = pl.program_id(0)
    for g in range(GROUP):
        sl = slice(g * HEAD_DIM, (g + 1) * HEAD_DIM)
        sink2 = jnp.full((q_ref.shape[0], HEAD_DIM), sink_ref[h * GROUP + g] * LOG2E, F32)
        o_ref[:, sl] = _sink_attend(_scores2(q_ref[:, sl], k_ref[...]), v_ref[...], sink2).astype(BF16)


def _ctx_attention(qkv_c, sink):
    l_ctx = qkv_c.shape[0]
    d_q = N_KV_HEADS * GROUP * HEAD_DIM
    kcol = d_q // HEAD_DIM
    vcol = kcol + N_KV_HEADS
    return pl.pallas_call(
        _ctx_attn_kernel,
        grid=(N_KV_HEADS,),
        in_specs=[
            pl.BlockSpec(memory_space=pltpu.SMEM),
            pl.BlockSpec((l_ctx, GROUP * HEAD_DIM), lambda h: (0, h)),
            pl.BlockSpec((l_ctx, HEAD_DIM), lambda h: (0, kcol + h)),
            pl.BlockSpec((l_ctx, HEAD_DIM), lambda h: (0, vcol + h)),
        ],
        out_specs=pl.BlockSpec((l_ctx, GROUP * HEAD_DIM), lambda h: (0, h)),
        out_shape=jax.ShapeDtypeStruct((l_ctx, d_q), BF16),
        compiler_params=_params("arbitrary"),
        name="ctx_attention",
    )(sink, qkv_c, qkv_c, qkv_c)


def _conv_in_kernel(h_ref, hp_ref, hn_ref, wb_ref, wc_ref, wv_ref, cw_ref, *rest):
    if len(rest) == 4:
        side_in_ref, u_ref, side_out_ref, hs_ref = rest
        side_out_ref[...] = side_in_ref[...].astype(BF16)
    else:
        u_ref, hs_ref = rest

    @pl.when(pl.program_id(1) == 0)
    def _():
        _fill_halo_lhs(h_ref, hp_ref, hn_ref, hs_ref)

    wb = wb_ref[...].astype(BF16)
    wc = wc_ref[...].astype(BF16)
    wv = wv_ref[...].astype(BF16)
    def finish(c0, rc, p_conv):
        gate_b = _dot(hs_ref[c0 + HALO_ROWS:c0 + HALO_ROWS + rc, :], wb)
        u_ref[c0:c0 + rc, :] = (gate_b * p_conv).astype(BF16)

    _chunked_conv3(hs_ref, h_ref.shape[0], lambda lhs: _dot(lhs, wc) * _dot(lhs, wv), cw_ref, finish)


def _conv_in(h, w_in, jl, conv_w, tm, side_w=None, tn=256):
    m, d = h.shape
    nt = d // tn

    def wspec(part):
        return pl.BlockSpec((None, d, tn), lambda i, j: (jl, 0, part * nt + j))

    in_specs = _halo_lhs_specs(tm, d, m) + [wspec(0), wspec(1), wspec(2),
                                            pl.BlockSpec((None, 3, tn), lambda i, j: (jl, 0, j))]
    args = [h, h, h, w_in, w_in, w_in, conv_w]
    out_specs = [pl.BlockSpec((tm, tn), lambda i, j: (i, j))]
    out_shape = [jax.ShapeDtypeStruct((m, d), BF16)]
    if side_w is not None:
        side = _SideCast(side_w, jl, (m // tm) * nt, lambda i, j: i * nt + j)
        in_specs.append(side.in_spec)
        args.append(side.arg)
        out_specs.append(side.out_spec)
        out_shape.append(side.out_shape)
    outs = pl.pallas_call(
        _conv_in_kernel,
        grid=(m // tm, nt),
        in_specs=in_specs,
        out_specs=out_specs,
        out_shape=out_shape,
        scratch_shapes=[pltpu.VMEM((tm + 2 * HALO_ROWS, d), BF16)],
        compiler_params=_params("arbitrary", "arbitrary"),
        name="conv_in",
    )(*args)
    return (outs[0], outs[1]) if side_w is not None else (outs[0], None)


def _ffn_up_kernel(h_ref, hp_ref, hn_ref, wg_ref, wv_ref, cw_ref, cb_ref, *rest):
    if len(rest) == 4:
        side_in_ref, a_ref, side_out_ref, hs_ref = rest
        side_out_ref[...] = side_in_ref[...].astype(BF16)
    else:
        a_ref, hs_ref = rest

    @pl.when(pl.program_id(1) == 0)
    def _():
        _fill_halo_lhs(h_ref, hp_ref, hn_ref, hs_ref)

    wg = wg_ref[...].astype(BF16)
    wv = wv_ref[...].astype(BF16)

    def finish(c0, rc, gate_conv):
        val = _dot(hs_ref[c0 + HALO_ROWS:c0 + HALO_ROWS + rc, :], wv)
        a_ref[c0:c0 + rc, :] = (jax.nn.silu(gate_conv + cb_ref[...]) * val).astype(BF16)

    _chunked_conv3(hs_ref, h_ref.shape[0], lambda lhs: _dot(lhs, wg), cw_ref, finish)


def _ffn_up(h, w_up, layer, conv_w, conv_b, tm, side_w=None, tn=256):
    m, d = h.shape
    f = w_up.shape[2] // 2
    nt = f // tn
    in_specs = _halo_lhs_specs(tm, d, m) + [
        pl.BlockSpec((None, d, tn), lambda i, j: (layer, 0, j)),
        pl.BlockSpec((None, d, tn), lambda i, j: (layer, 0, nt + j)),
        pl.BlockSpec((None, 3, tn), lambda i, j: (layer, 0, j)),
        pl.BlockSpec((None, 1, tn), lambda i, j: (layer, 0, j)),
    ]
    args = [h, h, h, w_up, w_up, conv_w, conv_b.reshape(conv_b.shape[0], 1, f)]
    out_specs = [pl.BlockSpec((tm, tn), lambda i, j: (i, j))]
    out_shape = [jax.ShapeDtypeStruct((m, f), BF16)]
    if side_w is not None:
        side = _SideCast(side_w, layer, (m // tm) * nt, lambda i, j: i * nt + j)
        in_specs.append(side.in_spec)
        args.append(side.arg)
        out_specs.append(side.out_spec)
        out_shape.append(side.out_shape)
    outs = pl.pallas_call(
        _ffn_up_kernel,
        grid=(m // tm, nt),
        in_specs=in_specs,
        out_specs=out_specs,
        out_shape=out_shape,
        scratch_shapes=[pltpu.VMEM((tm + 2 * HALO_ROWS, d), BF16)],
        compiler_params=_params("arbitrary", "arbitrary"),
        name="ffn_up",
    )(*args)
    return (outs[0], outs[1]) if side_w is not None else (outs[0], None)


def _out_proj_kernel(*refs, row, has_next, n_k):
    if has_next:
        a_ref, w_ref, x_ref, g_ref, sh_ref, sc_ref, o_ref, hn_ref = refs
    else:
        a_ref, w_ref, x_ref, g_ref, o_ref = refs
    k = pl.program_id(1)
    last = n_k - 1
    tm = a_ref.shape[0]
    rc = min(OUT_ROW_CHUNK, tm)
    chunks = [slice(c0, c0 + rc) for c0 in range(0, tm, rc)]

    def finish(rows, acc):
        xn = x_ref[rows, :] + g_ref[row:row + 1, :] * acc
        o_ref[rows, :] = xn
        if has_next:
            hn_ref[rows, :] = _modulated(xn, sh_ref, sc_ref, row)

    if n_k == 1:
        for rows in chunks:
            finish(rows, _dot(a_ref[rows, :], w_ref[...]))
        return

    @pl.when(k == 0)
    def _():
        for rows in chunks:
            o_ref[rows, :] = _dot(a_ref[rows, :], w_ref[...])

    @pl.when((k > 0) & (k < last))
    def _():
        for rows in chunks:
            o_ref[rows, :] += _dot(a_ref[rows, :], w_ref[...])

    @pl.when(k == last)
    def _():
        for rows in chunks:
            finish(rows, o_ref[rows, :] + _dot(a_ref[rows, :], w_ref[...]))


def _out_proj(a, w_bf16, jl, x, mod, layer, gate_chunk, nxt, row, tm, tk=512):
    m, d = x.shape
    kdim = a.shape[1]
    in_specs = [
        pl.BlockSpec((tm, tk), lambda i, k: (i, k)),
        pl.BlockSpec((None, tk, d), lambda i, k: (jl, k, 0)),
        pl.BlockSpec((tm, d), lambda i, k: (i, 0)),
        _mod_spec(layer, gate_chunk, d),
    ]
    args = [a, w_bf16, x, mod]
    out_specs = [pl.BlockSpec((tm, d), lambda i, k: (i, 0))]
    out_shape = [jax.ShapeDtypeStruct((m, d), F32)]
    if nxt is not None:
        in_specs += [_mod_spec(nxt[0], nxt[1], d), _mod_spec(nxt[0], nxt[2], d)]
        args += [mod, mod]
        out_specs.append(pl.BlockSpec((tm, d), lambda i, k: (i, 0)))
        out_shape.append(jax.ShapeDtypeStruct((m, d), BF16))
    outs = pl.pallas_call(
        functools.partial(_out_proj_kernel, row=row, has_next=nxt is not None, n_k=kdim // tk),
        grid=(m // tm, kdim // tk),
        in_specs=in_specs,
        out_specs=out_specs,
        out_shape=out_shape,
        compiler_params=_params("arbitrary", "arbitrary"),
        name="out_proj",
    )(*args)
    return (outs[0], outs[1]) if nxt is not None else (outs[0], None)


def _out_proj_resident_kernel(*refs, row, has_next):
    if has_next:
        a_ref, w_ref, x_ref, g_ref, sh_ref, sc_ref, o_ref, hn_ref = refs
    else:
        a_ref, w_ref, x_ref, g_ref, o_ref = refs
    tm = a_ref.shape[0]
    rc = min(OUT_ROW_CHUNK, tm)
    for c0 in range(0, tm, rc):
        rows = slice(c0, c0 + rc)
        xn = x_ref[rows, :] + g_ref[row:row + 1, :] * _dot(a_ref[rows, :], w_ref[...])
        o_ref[rows, :] = xn
        if has_next:
            hn_ref[rows, :] = _modulated(xn, sh_ref, sc_ref, row)


def _out_proj_resident(a, w_bf16, jl, x, mod, layer, gate_chunk, nxt, row, tm):
    m, d = x.shape
    kdim = a.shape[1]
    in_specs = [
        pl.BlockSpec((tm, kdim), lambda i: (i, 0)),
        pl.BlockSpec((None, kdim, d), lambda i: (jl, 0, 0), pipeline_mode=pl.Buffered(1)),
        pl.BlockSpec((tm, d), lambda i: (i, 0)),
        _mod_spec(layer, gate_chunk, d),
    ]
    args = [a, w_bf16, x, mod]
    out_specs = [pl.BlockSpec((tm, d), lambda i: (i, 0))]
    out_shape = [jax.ShapeDtypeStruct((m, d), F32)]
    if nxt is not None:
        in_specs += [_mod_spec(nxt[0], nxt[1], d), _mod_spec(nxt[0], nxt[2], d)]
        args += [mod, mod]
        out_specs.append(pl.BlockSpec((tm, d), lambda i: (i, 0)))
        out_shape.append(jax.ShapeDtypeStruct((m, d), BF16))
    outs = pl.pallas_call(
        functools.partial(_out_proj_resident_kernel, row=row, has_next=nxt is not None),
        grid=(m // tm,),
        in_specs=in_specs,
        out_specs=out_specs,
        out_shape=out_shape,
        compiler_params=_params("arbitrary"),
        name="out_proj_resident",
    )(*args)
    return (outs[0], outs[1]) if nxt is not None else (outs[0], None)


def _rope_tables(n):
    rows = n // GRID_W
    inv = ROPE_BASE ** (-jnp.arange(ROPE_PAIRS, dtype=F32) / ROPE_PAIRS)
    row_ang = jnp.arange(rows, dtype=F32)[:, None] * inv
    col_ang = jnp.arange(GRID_W, dtype=F32)[:, None] * inv

    def per_token(row_tab, col_tab, lo_sign):
        r = jnp.repeat(row_tab, GRID_W, axis=0)
        c = jnp.tile(col_tab, (rows, 1))
        return jnp.concatenate([lo_sign * r, lo_sign * c, r, c], axis=1)

    return per_token(jnp.cos(row_ang), jnp.cos(col_ang), 1.0), per_token(jnp.sin(row_ang), jnp.sin(col_ang), -1.0)


def kernel(x, c, ctx, c_ctx, w_ada, b_ada, attn_w_qkv, attn_w_o, attn_q_gain, attn_k_gain, attn_sink,
           sc_w_in, sc_conv, sc_w_out, ffn_w_up, ffn_conv, ffn_conv_b, ffn_w_down):
    batch, n, d = x.shape
    l_ctx = ctx.shape[1]
    depth = w_ada.shape[0]
    assert batch == 1 and d == N_KV_HEADS * GROUP * HEAD_DIM
    tm_in = min(2048, n)
    tm_out = min(1024, n)
    tm_res = min(512, n)
    tq = min(2048, n)
    tm_c = l_ctx

    cc = jnp.concatenate([c, c_ctx[None, :], jnp.zeros((MOD_ROWS - 2, d), F32)], axis=0)
    mod = _ada_table(cc, w_ada, b_ada)
    rope_tabs = _rope_tables(n)
    n_qk_heads = N_KV_HEADS * GROUP + N_KV_HEADS
    w_qkv = _paired_qkv_weight(attn_w_qkv, n_qk_heads * HEAD_DIM)
    xs, cs = x[0], ctx[0]
    hx = hc = None

    for l in range(depth):
        is_attn = (l % N_MIXERS) == 0
        j = l // N_MIXERS
        need_ctx = l < depth - 1
        nxt_ffn = (l, SHIFT_FFN, SCALE_FFN)
        nxt_mix = (l + 1, SHIFT_MIX, SCALE_MIX) if l + 1 < depth else None
        if is_attn:
            gains = _paired_lanes(jnp.stack([attn_q_gain[j], attn_k_gain[j]]), 1)
            if l == 0:
                qkv, w_o = _qkv_proj(xs, w_qkv, j, gains, rope_tabs, tm_res, side_w=attn_w_o,
                                     premod=(mod, l, ROW_LATENT))
                qkv_c, _ = _qkv_proj(cs, w_qkv, j, gains, None, tm_c, premod=(mod, l, ROW_CTX))
            else:
                qkv, w_o = _qkv_proj(hx, w_qkv, j, gains, rope_tabs, tm_res, side_w=attn_w_o)
                qkv_c, _ = _qkv_proj(hc, w_qkv, j, gains, None, tm_c)
            o = _band_attention(qkv, qkv_c, attn_sink[j], tq)
            xs, hx = _out_proj_resident(o, w_o[None], 0, xs, mod, l, GATE_MIX, nxt_ffn, ROW_LATENT, tm_res)
            if need_ctx:
                o_c = _ctx_attention(qkv_c, attn_sink[j])
                cs, hc = _out_proj_resident(o_c, w_o[None], 0, cs, mod, l, GATE_MIX, nxt_ffn, ROW_CTX, tm_c)
        else:
            u, w_out = _conv_in(hx, sc_w_in, j, sc_conv, tm_in, side_w=sc_w_out)
            xs, hx = _out_proj_resident(u, w_out[None], 0, xs, mod, l, GATE_MIX, nxt_ffn, ROW_LATENT, tm_res)
            if need_ctx:
                u_c, _ = _conv_in(hc, sc_w_in, j, sc_conv, tm_c, tn=CTX_TILE)
                cs, hc = _out_proj_resident(u_c, w_out[None], 0, cs, mod, l, GATE_MIX, nxt_ffn, ROW_CTX, tm_c)
        a, w_down = _ffn_up(hx, ffn_w_up, l, ffn_conv, ffn_conv_b, tm_in, side_w=ffn_w_down)
        xs, hx = _out_proj(a, w_down[None], 0, xs, mod, l, GATE_FFN, nxt_mix, ROW_LATENT, tm_out)
        if need_ctx:
            a_c, _ = _ffn_up(hc, ffn_w_up, l, ffn_conv, ffn_conv_b, tm_c, tn=CTX_TILE)
            f_half = a_c.shape[1] // 2
            tk_c = f_half if f_half % HEAD_DIM == 0 else CTX_TILE
            cs, hc = _out_proj(a_c, w_down[None], 0, cs, mod, l, GATE_FFN, nxt_mix, ROW_CTX, tm_c, tk=tk_c)
    return xs[None]
```

```python
import functools

import jax
import jax.numpy as jnp
from jax import lax
from jax.experimental import pallas as pl
from jax.experimental.pallas import tpu as pltpu

HEAD_DIM = 128
N_KV_HEADS = 4
GROUP = 4
WINDOW = 128
GRID_W = 64
ROPE_PAIRS = HEAD_DIM // 4
ROPE_BASE = 10000.0
N_MIXERS = 2
EPS = 1e-6
NEG_INF = -1e30

BF16 = jnp.bfloat16
F32 = jnp.float32

V7X_VMEM_LIMIT_BYTES = 56 * 1024 * 1024
HALO_ROWS = 16
F32_ROWS = 8
MOD_ROWS = 8
ROW_LATENT = 0
ROW_CTX = 1
ROW_CHUNK = 512
CTX_TILE = 512
OUT_ROW_CHUNK = 256
LOG2E = 1.4426950408889634
SCORE_SCALE2 = HEAD_DIM ** -0.5 * LOG2E
SHIFT_MIX, SCALE_MIX, GATE_MIX, SHIFT_FFN, SCALE_FFN, GATE_FFN = range(6)


def _params(*sem):
    return pltpu.CompilerParams(dimension_semantics=sem, vmem_limit_bytes=V7X_VMEM_LIMIT_BYTES)


def _dot(a, b):
    return jnp.dot(a, b, preferred_element_type=F32)


def _ada_kernel(cc_ref, w_ref, b_ref, o_ref):
    a = jax.nn.silu(cc_ref[...]).astype(BF16)
    o_ref[...] = _dot(a, w_ref[...].astype(BF16)) + b_ref[...]


def _ada_table(cc, w_ada, b_ada, tn=1024):
    depth, d, d6 = w_ada.shape
    return pl.pallas_call(
        _ada_kernel,
        grid=(depth, d6 // tn),
        in_specs=[
            pl.BlockSpec((MOD_ROWS, d), lambda l, j: (0, 0)),
            pl.BlockSpec((None, d, tn), lambda l, j: (l, 0, j)),
            pl.BlockSpec((None, 1, tn), lambda l, j: (l, 0, j)),
        ],
        out_specs=pl.BlockSpec((None, MOD_ROWS, tn), lambda l, j: (l, 0, j)),
        out_shape=jax.ShapeDtypeStruct((depth, MOD_ROWS, d6), F32),
        compiler_params=_params("arbitrary", "arbitrary"),
        name="ada_table",
    )(cc, w_ada, b_ada.reshape(depth, 1, d6))


def _modulated(x, sh_ref, sc_ref, row):
    r = lax.rsqrt(jnp.mean(x * x, axis=-1, keepdims=True) + EPS)
    return ((x * r) * (1 + sc_ref[row:row + 1, :]) + sh_ref[row:row + 1, :]).astype(BF16)


def _mod_spec(layer, chunk, d):
    return pl.BlockSpec((None, MOD_ROWS, d), lambda *_: (layer, 0, chunk))


def _fill_halo_lhs(h_ref, hp_ref, hn_ref, hs_ref):
    i = pl.program_id(0)
    tm = h_ref.shape[0]
    zero = jnp.zeros(hp_ref.shape, BF16)
    hs_ref[0:HALO_ROWS, :] = jnp.where(i > 0, hp_ref[...], zero)
    hs_ref[HALO_ROWS:HALO_ROWS + tm, :] = h_ref[...]
    hs_ref[HALO_ROWS + tm:, :] = jnp.where(i < pl.num_programs(0) - 1, hn_ref[...], zero)


def _halo_lhs_specs(tm, d, m):
    per = tm // HALO_ROWS
    nblk = m // HALO_ROWS
    return [
        pl.BlockSpec((tm, d), lambda i, j: (i, 0)),
        pl.BlockSpec((HALO_ROWS, d), lambda i, j: (jnp.maximum(i * per - 1, 0), 0)),
        pl.BlockSpec((HALO_ROWS, d), lambda i, j: (jnp.minimum((i + 1) * per, nblk - 1), 0)),
    ]


class _SideCast:
    def __init__(self, w, layer, n_steps, step_of):
        _, kdim, d = w.shape
        rows = kdim // n_steps
        assert kdim % n_steps == 0 and rows % HALO_ROWS == 0
        self.arg = w
        self.in_spec = pl.BlockSpec((None, rows, d), lambda *g: (layer, step_of(*g), 0))
        self.out_spec = pl.BlockSpec((rows, d), lambda *g: (step_of(*g), 0))
        self.out_shape = jax.ShapeDtypeStruct((kdim, d), BF16)

    @staticmethod
    def fits(w, n_steps):
        kdim = w.shape[1]
        return kdim % n_steps == 0 and (kdim // n_steps) % HALO_ROWS == 0


def _row_chunks(tm):
    rc = min(ROW_CHUNK, tm)
    return [(c0, rc) for c0 in range(0, tm, rc)]


def _chunked_conv3(hs_ref, tm, project, w_ref, finish):
    chunks = _row_chunks(tm)
    last = len(chunks) - 1
    ys = []

    def body(ci):
        off = HALO_ROWS if ci == 0 else 0
        return ys[ci][off:off + chunks[ci][1]]

    def emit(ci):
        c0, rc = chunks[ci]
        before = ys[0][HALO_ROWS - F32_ROWS:HALO_ROWS] if ci == 0 else body(ci - 1)[-F32_ROWS:]
        if ci == last:
            off = HALO_ROWS if ci == 0 else 0
            after = ys[ci][off + rc:off + rc + F32_ROWS]
        else:
            after = body(ci + 1)[:F32_ROWS]
        ext = jnp.concatenate([before, body(ci), after], axis=0)
        sl = slice(F32_ROWS, F32_ROWS + rc)
        dn = pltpu.roll(ext, 1, 0)[sl]
        up = pltpu.roll(ext, ext.shape[0] - 1, 0)[sl]
        finish(c0, rc, dn * w_ref[0:1, :] + ext[sl] * w_ref[1:2, :] + up * w_ref[2:3, :])

    for ci, (c0, rc) in enumerate(chunks):
        lo = c0 + (0 if ci == 0 else HALO_ROWS)
        hi = c0 + rc + HALO_ROWS + (HALO_ROWS if ci == last else 0)
        ys.append(project(hs_ref[lo:hi, :]))
        if ci >= 1:
            emit(ci - 1)
    emit(last)


def _paired_lanes(a, n_heads):
    lead = a.shape[:-1]
    return a.reshape(*lead, n_heads, 2, 2, ROPE_PAIRS).swapaxes(-3, -2).reshape(*lead, n_heads * HEAD_DIM)


def _pair_weight_kernel(w_ref, p_ref, o_ref):
    o_ref[...] = _dot(w_ref[...].astype(BF16), p_ref[...]).astype(BF16)


def _paired_qkv_weight(w_qkv, n_qk_cols, tn=512):
    layers, d, n_out = w_qkv.shape
    eye = jnp.eye(tn, dtype=BF16)
    perms = jnp.stack([_paired_lanes(eye, tn // HEAD_DIM), eye])
    n_qk_tiles = n_qk_cols // tn
    return pl.pallas_call(
        _pair_weight_kernel,
        grid=(layers, n_out // tn),
        in_specs=[
            pl.BlockSpec((None, d, tn), lambda l, j: (l, 0, j)),
            pl.BlockSpec((None, tn, tn), lambda l, j: (jnp.where(j < n_qk_tiles, 0, 1), 0, 0)),
        ],
        out_specs=pl.BlockSpec((None, d, tn), lambda l, j: (l, 0, j)),
        out_shape=jax.ShapeDtypeStruct((layers, d, n_out), BF16),
        compiler_params=_params("arbitrary", "arbitrary"),
        name="pair_qkv_weight",
    )(w_qkv, perms)


def _qkv_kernel(*refs, rope, side, premod_row, tn, n_q_tiles, n_qk_tiles):
    refs = list(refs)
    if side:
        side_out_ref = refs.pop()
        side_in_ref = refs.pop(-2)
        side_out_ref[...] = side_in_ref[...].astype(BF16)
    if premod_row is None:
        h = refs.pop(0)[...]
    else:
        x_ref, sh_ref, sc_ref = refs[:3]
        del refs[:3]
        h = _modulated(x_ref[...], sh_ref, sc_ref, premod_row)
    if rope:
        w_ref, gain_ref, cos_ref, sin_ref, o_ref = refs
    else:
        w_ref, gain_ref, o_ref = refs
    n_tiles = w_ref.shape[1] // tn
    y_next = _dot(h, w_ref[:, 0:tn])
    for jt in range(n_tiles):
        y, y_next = y_next, (_dot(h, w_ref[:, (jt + 1) * tn:(jt + 2) * tn]) if jt + 1 < n_tiles else None)
        if jt >= n_qk_tiles:
            o_ref[:, jt * tn:(jt + 1) * tn] = y.astype(BF16)
            continue
        gain = gain_ref[0:1, :] * SCORE_SCALE2 if jt < n_q_tiles else gain_ref[1:2, :]
        for hh in range(tn // HEAD_DIM):
            yh = y[:, hh * HEAD_DIM:(hh + 1) * HEAD_DIM]
            r = lax.rsqrt(jnp.mean(yh * yh, axis=-1, keepdims=True) + EPS)
            yh = (yh * r) * gain
            if rope:
                yh = yh * cos_ref[...] + pltpu.roll(yh, HEAD_DIM // 2, 1) * sin_ref[...]
            o_ref[:, jt * tn + hh * HEAD_DIM:jt * tn + (hh + 1) * HEAD_DIM] = yh.astype(BF16)


def _qkv_proj(h, w_qkv_paired, jl, gains_paired, rope_tabs, tm, side_w=None, premod=None, tn=512):
    m, d = h.shape
    n_out = w_qkv_paired.shape[2]
    d_q = d
    d_kv = (n_out - d_q) // 2
    rope = rope_tabs is not None
    in_specs = [pl.BlockSpec((tm, d), lambda i: (i, 0))]
    args = [h]
    if premod is not None:
        mod, layer, _ = premod
        in_specs += [_mod_spec(layer, SHIFT_MIX, d), _mod_spec(layer, SCALE_MIX, d)]
        args += [mod, mod]
    in_specs += [
        pl.BlockSpec((None, d, n_out), lambda i: (jl, 0, 0), pipeline_mode=pl.Buffered(1)),
        pl.BlockSpec((2, HEAD_DIM), lambda i: (0, 0)),
    ]
    args += [w_qkv_paired, gains_paired]
    if rope:
        in_specs += [pl.BlockSpec((tm, HEAD_DIM), lambda i: (i, 0))] * 2
        args += list(rope_tabs)
    out_specs = [pl.BlockSpec((tm, n_out), lambda i: (i, 0))]
    out_shape = [jax.ShapeDtypeStruct((m, n_out), BF16)]
    if side_w is not None:
        side = _SideCast(side_w, jl, m // tm, lambda i: i)
        in_specs.append(side.in_spec)
        args.append(side.arg)
        out_specs.append(side.out_spec)
        out_shape.append(side.out_shape)
    kern = functools.partial(_qkv_kernel, rope=rope, side=side_w is not None,
                             premod_row=None if premod is None else premod[2], tn=tn,
                             n_q_tiles=d_q // tn, n_qk_tiles=(d_q + d_kv) // tn)
    outs = pl.pallas_call(
        kern,
        grid=(m // tm,),
        in_specs=in_specs,
        out_specs=out_specs,
        out_shape=out_shape,
        compiler_params=_params("arbitrary"),
        name="qkv_proj",
    )(*args)
    return (outs[0], outs[1]) if side_w is not None else (outs[0], None)


def _sink_attend(t, v, sink2):
    rows, n_keys = t.shape
    m2 = jnp.maximum(jnp.broadcast_to(jnp.max(t, axis=-1, keepdims=True), (rows, HEAD_DIM)), sink2)
    e = jnp.exp2(t - jnp.tile(m2, (1, n_keys // HEAD_DIM))).astype(BF16)
    ov = _dot(e, jnp.concatenate([v, jnp.ones_like(v)], axis=1))
    return ov[:, :HEAD_DIM] / (ov[:, HEAD_DIM:] + jnp.exp2(sink2 - m2))


def _scores2(q, k):
    return lax.dot_general(q, k, (((1,), (1,)), ((), ())), preferred_element_type=F32)


def _band_attn_kernel(sink_ref, q_ref, kp_ref, km_ref, kn_ref, kc_ref, vp_ref, vm_ref, vn_ref, vc_ref, o_ref,
                      kw_ref, vw_ref):
    h = pl.program_id(0)
    i = pl.program_id(1)
    tq = q_ref.shape[0]
    nb = tq // WINDOW
    last_blk = pl.num_programs(1) * nb - 1
    kw_ref[0:WINDOW, :] = kp_ref[...]
    kw_ref[WINDOW:WINDOW + tq, :] = km_ref[...]
    kw_ref[WINDOW + tq:, :] = kn_ref[...]
    vw_ref[0:WINDOW, :] = vp_ref[...]
    vw_ref[WINDOW:WINDOW + tq, :] = vm_ref[...]
    vw_ref[WINDOW + tq:, :] = vn_ref[...]
    rows = GROUP * WINDOW
    r = lax.broadcasted_iota(jnp.int32, (rows, WINDOW), 0) & (WINDOW - 1)
    c_minus_r = lax.broadcasted_iota(jnp.int32, (rows, WINDOW), 1) - r
    sink2 = jnp.concatenate([jnp.full((WINDOW, HEAD_DIM), sink_ref[h * GROUP + g] * LOG2E, F32)
                             for g in range(GROUP)], axis=0)
    kc = kc_ref[...]
    vc = vc_ref[...]

    def masked_scores(b):
        blk = i * nb + b
        q4 = jnp.concatenate([q_ref[b * WINDOW:(b + 1) * WINDOW, g * HEAD_DIM:(g + 1) * HEAD_DIM]
                              for g in range(GROUP)], axis=0)
        s = _scores2(q4, jnp.concatenate([kw_ref[b * WINDOW:(b + 3) * WINDOW, :], kc], axis=0))
        lo = jnp.where(blk > 0, 0, WINDOW)
        hi = jnp.where(blk < last_blk, 0, -WINDOW)
        s_prev = jnp.where(c_minus_r >= lo, s[:, 0:WINDOW], NEG_INF)
        s_next = jnp.where(c_minus_r <= hi, s[:, 2 * WINDOW:3 * WINDOW], NEG_INF)
        return jnp.concatenate([s_prev, s[:, WINDOW:2 * WINDOW], s_next, s[:, 3 * WINDOW:]], axis=1)

    ahead = 1
    scores = {b: masked_scores(b) for b in range(min(ahead, nb))}
    for b in range(nb):
        if b + ahead < nb:
            scores[b + ahead] = masked_scores(b + ahead)
        v = jnp.concatenate([vw_ref[b * WINDOW:(b + 3) * WINDOW, :], vc], axis=0)
        o = _sink_attend(scores.pop(b), v, sink2)
        for g in range(GROUP):
            o_ref[b * WINDOW:(b + 1) * WINDOW, g * HEAD_DIM:(g + 1) * HEAD_DIM] = (
                o[g * WINDOW:(g + 1) * WINDOW, :].astype(BF16))


def _band_attention(qkv, qkv_c, sink, tq):
    n = qkv.shape[0]
    l_ctx = qkv_c.shape[0]
    d_q = N_KV_HEADS * GROUP * HEAD_DIM
    kcol = d_q // HEAD_DIM
    vcol = kcol + N_KV_HEADS
    per = tq // WINDOW
    nblk = n // WINDOW

    def prev(col0):
        return pl.BlockSpec((WINDOW, HEAD_DIM), lambda h, i: (jnp.maximum(i * per - 1, 0), col0 + h))

    def main(col0):
        return pl.BlockSpec((tq, HEAD_DIM), lambda h, i: (i, col0 + h))

    def nxt(col0):
        return pl.BlockSpec((WINDOW, HEAD_DIM), lambda h, i: (jnp.minimum((i + 1) * per, nblk - 1), col0 + h))

    def ctx(col0):
        return pl.BlockSpec((l_ctx, HEAD_DIM), lambda h, i: (0, col0 + h))

    return pl.pallas_call(
        _band_attn_kernel,
        grid=(N_KV_HEADS, n // tq),
        in_specs=[
            pl.BlockSpec(memory_space=pltpu.SMEM),
            pl.BlockSpec((tq, GROUP * HEAD_DIM), lambda h, i: (i, h)),
            prev(kcol), main(kcol), nxt(kcol), ctx(kcol),
            prev(vcol), main(vcol), nxt(vcol), ctx(vcol),
        ],
        out_specs=pl.BlockSpec((tq, GROUP * HEAD_DIM), lambda h, i: (i, h)),
        out_shape=jax.ShapeDtypeStruct((n, d_q), BF16),
        scratch_shapes=[pltpu.VMEM((tq + 2 * WINDOW, HEAD_DIM), BF16)] * 2,
        compiler_params=_params("arbitrary", "arbitrary"),
        name="band_attention",
    )(sink, qkv, qkv, qkv, qkv, qkv_c, qkv, qkv, qkv, qkv_c)


def _ctx_attn_kernel(sink_ref, q_ref, k_ref, v_ref, o_ref):
    h = pl.program_id(0)
    for g in range(GROUP):
        sl = slice(g * HEAD_DIM, (g + 1) * HEAD_DIM)
        sink2 = jnp.full((q_ref.shape[0], HEAD_DIM), sink_ref[h * GROUP + g] * LOG2E, F32)
        o_ref[:, sl] = _sink_attend(_scores2(q_ref[:, sl], k_ref[...]), v_ref[...], sink2).astype(BF16)


def _ctx_attention(qkv_c, sink):
    l_ctx = qkv_c.shape[0]
    d_q = N_KV_HEADS * GROUP * HEAD_DIM
    kcol = d_q // HEAD_DIM
    vcol = kcol + N_KV_HEADS
    return pl.pallas_call(
        _ctx_attn_kernel,
        grid=(N_KV_HEADS,),
        in_specs=[
            pl.BlockSpec(memory_space=pltpu.SMEM),
            pl.BlockSpec((l_ctx, GROUP * HEAD_DIM), lambda h: (0, h)),
            pl.BlockSpec((l_ctx, HEAD_DIM), lambda h: (0, kcol + h)),
            pl.BlockSpec((l_ctx, HEAD_DIM), lambda h: (0, vcol + h)),
        ],
        out_specs=pl.BlockSpec((l_ctx, GROUP * HEAD_DIM), lambda h: (0, h)),
        out_shape=jax.ShapeDtypeStruct((l_ctx, d_q), BF16),
        compiler_params=_params("arbitrary"),
        name="ctx_attention",
    )(sink, qkv_c, qkv_c, qkv_c)


def _conv_in_kernel(h_ref, hp_ref, hn_ref, wb_ref, wc_ref, wv_ref, cw_ref, *rest):
    if len(rest) == 4:
        side_in_ref, u_ref, side_out_ref, hs_ref = rest
        side_out_ref[...] = side_in_ref[...].astype(BF16)
    else:
        u_ref, hs_ref = rest

    @pl.when(pl.program_id(1) == 0)
    def _():
        _fill_halo_lhs(h_ref, hp_ref, hn_ref, hs_ref)

    wb = wb_ref[...].astype(BF16)
    wc = wc_ref[...].astype(BF16)
    wv = wv_ref[...].astype(BF16)
    def finish(c0, rc, p_conv):
        gate_b = _dot(hs_ref[c0 + HALO_ROWS:c0 + HALO_ROWS + rc, :], wb)
        u_ref[c0:c0 + rc, :] = (gate_b * p_conv).astype(BF16)

    _chunked_conv3(hs_ref, h_ref.shape[0], lambda lhs: _dot(lhs, wc) * _dot(lhs, wv), cw_ref, finish)


def _conv_in(h, w_in, jl, conv_w, tm, side_w=None, tn=256):
    m, d = h.shape
    nt = d // tn

    def wspec(part):
        return pl.BlockSpec((None, d, tn), lambda i, j: (jl, 0, part * nt + j))

    in_specs = _halo_lhs_specs(tm, d, m) + [wspec(0), wspec(1), wspec(2),
                                            pl.BlockSpec((None, 3, tn), lambda i, j: (jl, 0, j))]
    args = [h, h, h, w_in, w_in, w_in, conv_w]
    out_specs = [pl.BlockSpec((tm, tn), lambda i, j: (i, j))]
    out_shape = [jax.ShapeDtypeStruct((m, d), BF16)]
    if side_w is not None:
        side = _SideCast(side_w, jl, (m // tm) * nt, lambda i, j: i * nt + j)
        in_specs.append(side.in_spec)
        args.append(side.arg)
        out_specs.append(side.out_spec)
        out_shape.append(side.out_shape)
    outs = pl.pallas_call(
        _conv_in_kernel,
        grid=(m // tm, nt),
        in_specs=in_specs,
        out_specs=out_specs,
        out_shape=out_shape,
        scratch_shapes=[pltpu.VMEM((tm + 2 * HALO_ROWS, d), BF16)],
        compiler_params=_params("arbitrary", "arbitrary"),
        name="conv_in",
    )(*args)
    return (outs[0], outs[1]) if side_w is not None else (outs[0], None)


def _ffn_up_kernel(h_ref, hp_ref, hn_ref, wg_ref, wv_ref, cw_ref, cb_ref, *rest):
    if len(rest) == 4:
        side_in_ref, a_ref, side_out_ref, hs_ref = rest
        side_out_ref[...] = side_in_ref[...].astype(BF16)
    else:
        a_ref, hs_ref = rest

    @pl.when(pl.program_id(1) == 0)
    def _():
        _fill_halo_lhs(h_ref, hp_ref, hn_ref, hs_ref)

    wg = wg_ref[...].astype(BF16)
    wv = wv_ref[...].astype(BF16)

    def finish(c0, rc, gate_conv):
        val = _dot(hs_ref[c0 + HALO_ROWS:c0 + HALO_ROWS + rc, :], wv)
        a_ref[c0:c0 + rc, :] = (jax.nn.silu(gate_conv + cb_ref[...]) * val).astype(BF16)

    _chunked_conv3(hs_ref, h_ref.shape[0], lambda lhs: _dot(lhs, wg), cw_ref, finish)


def _ffn_up(h, w_up, layer, conv_w, conv_b, tm, side_w=None, tn=256):
    m, d = h.shape
    f = w_up.shape[2] // 2
    nt = f // tn
    in_specs = _halo_lhs_specs(tm, d, m) + [
        pl.BlockSpec((None, d, tn), lambda i, j: (layer, 0, j)),
        pl.BlockSpec((None, d, tn), lambda i, j: (layer, 0, nt + j)),
        pl.BlockSpec((None, 3, tn), lambda i, j: (layer, 0, j)),
        pl.BlockSpec((None, 1, tn), lambda i, j: (layer, 0, j)),
    ]
    args = [h, h, h, w_up, w_up, conv_w, conv_b.reshape(conv_b.shape[0], 1, f)]
    out_specs = [pl.BlockSpec((tm, tn), lambda i, j: (i, j))]
    out_shape = [jax.ShapeDtypeStruct((m, f), BF16)]
    if side_w is not None:
        side = _SideCast(side_w, layer, (m // tm) * nt, lambda i, j: i * nt + j)
        in_specs.append(side.in_spec)
        args.append(side.arg)
        out_specs.append(side.out_spec)
        out_shape.append(side.out_shape)
    outs = pl.pallas_call(
        _ffn_up_kernel,
        grid=(m // tm, nt),
        in_specs=in_specs,
        out_specs=out_specs,
        out_shape=out_shape,
        scratch_shapes=[pltpu.VMEM((tm + 2 * HALO_ROWS, d), BF16)],
        compiler_params=_params("arbitrary", "arbitrary"),
        name="ffn_up",
    )(*args)
    return (outs[0], outs[1]) if side_w is not None else (outs[0], None)


def _out_proj_kernel(*refs, row, has_next, n_k):
    if has_next:
        a_ref, w_ref, x_ref, g_ref, sh_ref, sc_ref, o_ref, hn_ref = refs
    else:
        a_ref, w_ref, x_ref, g_ref, o_ref = refs
    k = pl.program_id(1)
    last = n_k - 1
    tm = a_ref.shape[0]
    rc = min(OUT_ROW_CHUNK, tm)
    chunks = [slice(c0, c0 + rc) for c0 in range(0, tm, rc)]

    def finish(rows, acc):
        xn = x_ref[rows, :] + g_ref[row:row + 1, :] * acc
        o_ref[rows, :] = xn
        if has_next:
            hn_ref[rows, :] = _modulated(xn, sh_ref, sc_ref, row)

    if n_k == 1:
        for rows in chunks:
            finish(rows, _dot(a_ref[rows, :], w_ref[...]))
        return

    @pl.when(k == 0)
    def _():
        for rows in chunks:
            o_ref[rows, :] = _dot(a_ref[rows, :], w_ref[...])

    @pl.when((k > 0) & (k < last))
    def _():
        for rows in chunks:
            o_ref[rows, :] += _dot(a_ref[rows, :], w_ref[...])

    @pl.when(k == last)
    def _():
        part_next = _dot(a_ref[chunks[0], :], w_ref[...])
        for ci, rows in enumerate(chunks):
            part, part_next = part_next, (_dot(a_ref[chunks[ci + 1], :], w_ref[...])
                                          if ci + 1 < len(chunks) else None)
            finish(rows, o_ref[rows, :] + part)


def _out_proj(a, w_bf16, jl, x, mod, layer, gate_chunk, nxt, row, tm, tk=512):
    m, d = x.shape
    kdim = a.shape[1]
    in_specs = [
        pl.BlockSpec((tm, tk), lambda i, k: (i, k)),
        pl.BlockSpec((None, tk, d), lambda i, k: (jl, k, 0)),
        pl.BlockSpec((tm, d), lambda i, k: (i, 0)),
        _mod_spec(layer, gate_chunk, d),
    ]
    args = [a, w_bf16, x, mod]
    out_specs = [pl.BlockSpec((tm, d), lambda i, k: (i, 0))]
    out_shape = [jax.ShapeDtypeStruct((m, d), F32)]
    if nxt is not None:
        in_specs += [_mod_spec(nxt[0], nxt[1], d), _mod_spec(nxt[0], nxt[2], d)]
        args += [mod, mod]
        out_specs.append(pl.BlockSpec((tm, d), lambda i, k: (i, 0)))
        out_shape.append(jax.ShapeDtypeStruct((m, d), BF16))
    outs = pl.pallas_call(
        functools.partial(_out_proj_kernel, row=row, has_next=nxt is not None, n_k=kdim // tk),
        grid=(m // tm, kdim // tk),
        in_specs=in_specs,
        out_specs=out_specs,
        out_shape=out_shape,
        compiler_params=_params("arbitrary", "arbitrary"),
        name="out_proj",
    )(*args)
    return (outs[0], outs[1]) if nxt is not None else (outs[0], None)


def _out_proj_resident_kernel(*refs, row, has_next):
    if has_next:
        a_ref, w_ref, x_ref, g_ref, sh_ref, sc_ref, o_ref, hn_ref = refs
    else:
        a_ref, w_ref, x_ref, g_ref, o_ref = refs
    tm = a_ref.shape[0]
    rc = min(OUT_ROW_CHUNK, tm)
    chunks = [slice(c0, c0 + rc) for c0 in range(0, tm, rc)]
    acc_next = _dot(a_ref[chunks[0], :], w_ref[...])
    for ci, rows in enumerate(chunks):
        acc, acc_next = acc_next, (_dot(a_ref[chunks[ci + 1], :], w_ref[...]) if ci + 1 < len(chunks) else None)
        xn = x_ref[rows, :] + g_ref[row:row + 1, :] * acc
        o_ref[rows, :] = xn
        if has_next:
            hn_ref[rows, :] = _modulated(xn, sh_ref, sc_ref, row)


def _out_proj_resident(a, w_bf16, jl, x, mod, layer, gate_chunk, nxt, row, tm):
    m, d = x.shape
    kdim = a.shape[1]
    in_specs = [
        pl.BlockSpec((tm, kdim), lambda i: (i, 0)),
        pl.BlockSpec((None, kdim, d), lambda i: (jl, 0, 0), pipeline_mode=pl.Buffered(1)),
        pl.BlockSpec((tm, d), lambda i: (i, 0)),
        _mod_spec(layer, gate_chunk, d),
    ]
    args = [a, w_bf16, x, mod]
    out_specs = [pl.BlockSpec((tm, d), lambda i: (i, 0))]
    out_shape = [jax.ShapeDtypeStruct((m, d), F32)]
    if nxt is not None:
        in_specs += [_mod_spec(nxt[0], nxt[1], d), _mod_spec(nxt[0], nxt[2], d)]
        args += [mod, mod]
        out_specs.append(pl.BlockSpec((tm, d), lambda i: (i, 0)))
        out_shape.append(jax.ShapeDtypeStruct((m, d), BF16))
    outs = pl.pallas_call(
        functools.partial(_out_proj_resident_kernel, row=row, has_next=nxt is not None),
        grid=(m // tm,),
        in_specs=in_specs,
        out_specs=out_specs,
        out_shape=out_shape,
        compiler_params=_params("arbitrary"),
        name="out_proj_resident",
    )(*args)
    return (outs[0], outs[1]) if nxt is not None else (outs[0], None)


def _rope_tables(n):
    rows = n // GRID_W
    inv = ROPE_BASE ** (-jnp.arange(ROPE_PAIRS, dtype=F32) / ROPE_PAIRS)
    row_ang = jnp.arange(rows, dtype=F32)[:, None] * inv
    col_ang = jnp.arange(GRID_W, dtype=F32)[:, None] * inv

    def per_token(row_tab, col_tab, lo_sign):
        r = jnp.repeat(row_tab, GRID_W, axis=0)
        c = jnp.tile(col_tab, (rows, 1))
        return jnp.concatenate([lo_sign * r, lo_sign * c, r, c], axis=1)

    return per_token(jnp.cos(row_ang), jnp.cos(col_ang), 1.0), per_token(jnp.sin(row_ang), jnp.sin(col_ang), -1.0)


def kernel(x, c, ctx, c_ctx, w_ada, b_ada, attn_w_qkv, attn_w_o, attn_q_gain, attn_k_gain, attn_sink,
           sc_w_in, sc_conv, sc_w_out, ffn_w_up, ffn_conv, ffn_conv_b, ffn_w_down):
    batch, n, d = x.shape
    l_ctx = ctx.shape[1]
    depth = w_ada.shape[0]
    assert batch == 1 and d == N_KV_HEADS * GROUP * HEAD_DIM
    tm_in = min(2048, n)
    tm_out = min(1024, n)
    tm_res = min(512, n)
    tq = min(2048, n)
    tm_c = l_ctx

    cc = jnp.concatenate([c, c_ctx[None, :], jnp.zeros((MOD_ROWS - 2, d), F32)], axis=0)
    mod = _ada_table(cc, w_ada, b_ada)
    rope_tabs = _rope_tables(n)
    n_qk_heads = N_KV_HEADS * GROUP + N_KV_HEADS
    w_qkv = _paired_qkv_weight(attn_w_qkv, n_qk_heads * HEAD_DIM)
    xs, cs = x[0], ctx[0]
    hx = hc = None

    for l in range(depth):
        is_attn = (l % N_MIXERS) == 0
        j = l // N_MIXERS
        need_ctx = l < depth - 1
        nxt_ffn = (l, SHIFT_FFN, SCALE_FFN)
        nxt_mix = (l + 1, SHIFT_MIX, SCALE_MIX) if l + 1 < depth else None
        if is_attn:
            gains = _paired_lanes(jnp.stack([attn_q_gain[j], attn_k_gain[j]]), 1)
            if l == 0:
                qkv, w_o = _qkv_proj(xs, w_qkv, j, gains, rope_tabs, tm_res, side_w=attn_w_o,
                                     premod=(mod, l, ROW_LATENT))
                qkv_c, _ = _qkv_proj(cs, w_qkv, j, gains, None, tm_c, premod=(mod, l, ROW_CTX))
            else:
                qkv, w_o = _qkv_proj(hx, w_qkv, j, gains, rope_tabs, tm_res, side_w=attn_w_o)
                qkv_c, _ = _qkv_proj(hc, w_qkv, j, gains, None, tm_c)
            o = _band_attention(qkv, qkv_c, attn_sink[j], tq)
            xs, hx = _out_proj_resident(o, w_o[None], 0, xs, mod, l, GATE_MIX, nxt_ffn, ROW_LATENT, tm_res)
            if need_ctx:
                o_c = _ctx_attention(qkv_c, attn_sink[j])
                cs, hc = _out_proj_resident(o_c, w_o[None], 0, cs, mod, l, GATE_MIX, nxt_ffn, ROW_CTX, tm_c)
        else:
            u, w_out = _conv_in(hx, sc_w_in, j, sc_conv, tm_in, side_w=sc_w_out)
            xs, hx = _out_proj_resident(u, w_out[None], 0, xs, mod, l, GATE_MIX, nxt_ffn, ROW_LATENT, tm_res)
            if need_ctx:
                u_c, _ = _conv_in(hc, sc_w_in, j, sc_conv, tm_c, tn=CTX_TILE)
                cs, hc = _out_proj_resident(u_c, w_out[None], 0, cs, mod, l, GATE_MIX, nxt_ffn, ROW_CTX, tm_c)
        a, w_down = _ffn_up(hx, ffn_w_up, l, ffn_conv, ffn_conv_b, tm_in, side_w=ffn_w_down)
        xs, hx = _out_proj(a, w_down[None], 0, xs, mod, l, GATE_FFN, nxt_mix, ROW_LATENT, tm_out)
        if need_ctx:
            a_c, _ = _ffn_up(hc, ffn_w_up, l, ffn_conv, ffn_conv_b, tm_c, tn=CTX_TILE)
            f_half = a_c.shape[1] // 2
            tk_c = f_half if f_half % HEAD_DIM == 0 else CTX_TILE
            cs, hc = _out_proj(a_c, w_down[None], 0, cs, mod, l, GATE_FFN, nxt_mix, ROW_CTX, tm_c, tk=tk_c)
    return xs[None]
```

```python
import functools

import jax
import jax.numpy as jnp
from jax import lax
from jax.experimental import pallas as pl
from jax.experimental.pallas import tpu as pltpu

HEAD_DIM = 128
N_KV_HEADS = 4
GROUP = 4
WINDOW = 128
GRID_W = 64
ROPE_PAIRS = HEAD_DIM // 4
ROPE_BASE = 10000.0
N_MIXERS = 2
EPS = 1e-6
NEG_INF = -1e30

BF16 = jnp.bfloat16
F32 = jnp.float32

V7X_VMEM_LIMIT_BYTES = 56 * 1024 * 1024
HALO_ROWS = 16
F32_ROWS = 8
MOD_ROWS = 8
ROW_LATENT = 0
ROW_CTX = 1
ROW_CHUNK = 512
CTX_TILE = 512
OUT_ROW_CHUNK = 256
LOG2E = 1.4426950408889634
SCORE_SCALE2 = HEAD_DIM ** -0.5 * LOG2E
SHIFT_MIX, SCALE_MIX, GATE_MIX, SHIFT_FFN, SCALE_FFN, GATE_FFN = range(6)


def _params(*sem):
    return pltpu.CompilerParams(dimension_semantics=sem, vmem_limit_bytes=V7X_VMEM_LIMIT_BYTES)


def _dot(a, b):
    return jnp.dot(a, b, preferred_element_type=F32)


def _ada_kernel(cc_ref, w_ref, b_ref, o_ref):
    a = jax.nn.silu(cc_ref[...]).astype(BF16)
    o_ref[...] = _dot(a, w_ref[...].astype(BF16)) + b_ref[...]


def _ada_table(cc, w_ada, b_ada, tn=1024):
    depth, d, d6 = w_ada.shape
    return pl.pallas_call(
        _ada_kernel,
        grid=(depth, d6 // tn),
        in_specs=[
            pl.BlockSpec((MOD_ROWS, d), lambda l, j: (0, 0)),
            pl.BlockSpec((None, d, tn), lambda l, j: (l, 0, j)),
            pl.BlockSpec((None, 1, tn), lambda l, j: (l, 0, j)),
        ],
        out_specs=pl.BlockSpec((None, MOD_ROWS, tn), lambda l, j: (l, 0, j)),
        out_shape=jax.ShapeDtypeStruct((depth, MOD_ROWS, d6), F32),
        compiler_params=_params("arbitrary", "arbitrary"),
        name="ada_table",
    )(cc, w_ada, b_ada.reshape(depth, 1, d6))


def _modulated(x, sh_ref, sc_ref, row):
    r = lax.rsqrt(jnp.mean(x * x, axis=-1, keepdims=True) + EPS)
    return ((x * r) * (1 + sc_ref[row:row + 1, :]) + sh_ref[row:row + 1, :]).astype(BF16)


def _mod_spec(layer, chunk, d):
    return pl.BlockSpec((None, MOD_ROWS, d), lambda *_: (layer, 0, chunk))


def _fill_halo_lhs(h_ref, hp_ref, hn_ref, hs_ref):
    i = pl.program_id(0)
    tm = h_ref.shape[0]
    zero = jnp.zeros(hp_ref.shape, BF16)
    hs_ref[0:HALO_ROWS, :] = jnp.where(i > 0, hp_ref[...], zero)
    hs_ref[HALO_ROWS:HALO_ROWS + tm, :] = h_ref[...]
    hs_ref[HALO_ROWS + tm:, :] = jnp.where(i < pl.num_programs(0) - 1, hn_ref[...], zero)


def _halo_lhs_specs(tm, d, m):
    per = tm // HALO_ROWS
    nblk = m // HALO_ROWS
    return [
        pl.BlockSpec((tm, d), lambda i, j: (i, 0)),
        pl.BlockSpec((HALO_ROWS, d), lambda i, j: (jnp.maximum(i * per - 1, 0), 0)),
        pl.BlockSpec((HALO_ROWS, d), lambda i, j: (jnp.minimum((i + 1) * per, nblk - 1), 0)),
    ]


class _SideCast:
    def __init__(self, w, layer, n_steps, step_of):
        _, kdim, d = w.shape
        rows = kdim // n_steps
        assert kdim % n_steps == 0 and rows % HALO_ROWS == 0
        self.arg = w
        self.in_spec = pl.BlockSpec((None, rows, d), lambda *g: (layer, step_of(*g), 0))
        self.out_spec = pl.BlockSpec((rows, d), lambda *g: (step_of(*g), 0))
        self.out_shape = jax.ShapeDtypeStruct((kdim, d), BF16)

    @staticmethod
    def fits(w, n_steps):
        kdim = w.shape[1]
        return kdim % n_steps == 0 and (kdim // n_steps) % HALO_ROWS == 0


def _row_chunks(tm):
    rc = min(ROW_CHUNK, tm)
    return [(c0, rc) for c0 in range(0, tm, rc)]


def _chunked_conv3(hs_ref, tm, project, w_ref, finish):
    chunks = _row_chunks(tm)
    last = len(chunks) - 1
    ys = []

    def body(ci):
        off = HALO_ROWS if ci == 0 else 0
        return ys[ci][off:off + chunks[ci][1]]

    def emit(ci):
        c0, rc = chunks[ci]
        before = ys[0][HALO_ROWS - F32_ROWS:HALO_ROWS] if ci == 0 else body(ci - 1)[-F32_ROWS:]
        if ci == last:
            off = HALO_ROWS if ci == 0 else 0
            after = ys[ci][off + rc:off + rc + F32_ROWS]
        else:
            after = body(ci + 1)[:F32_ROWS]
        ext = jnp.concatenate([before, body(ci), after], axis=0)
        sl = slice(F32_ROWS, F32_ROWS + rc)
        dn = pltpu.roll(ext, 1, 0)[sl]
        up = pltpu.roll(ext, ext.shape[0] - 1, 0)[sl]
        finish(c0, rc, dn * w_ref[0:1, :] + ext[sl] * w_ref[1:2, :] + up * w_ref[2:3, :])

    for ci, (c0, rc) in enumerate(chunks):
        lo = c0 + (0 if ci == 0 else HALO_ROWS)
        hi = c0 + rc + HALO_ROWS + (HALO_ROWS if ci == last else 0)
        ys.append(project(hs_ref[lo:hi, :]))
        if ci >= 1:
            emit(ci - 1)
    emit(last)


def _paired_lanes(a, n_heads):
    lead = a.shape[:-1]
    return a.reshape(*lead, n_heads, 2, 2, ROPE_PAIRS).swapaxes(-3, -2).reshape(*lead, n_heads * HEAD_DIM)


def _pair_weight_kernel(w_ref, p_ref, o_ref):
    o_ref[...] = _dot(w_ref[...].astype(BF16), p_ref[...]).astype(BF16)


def _paired_qkv_weight(w_qkv, n_qk_cols, tn=512):
    layers, d, n_out = w_qkv.shape
    eye = jnp.eye(tn, dtype=BF16)
    perms = jnp.stack([_paired_lanes(eye, tn // HEAD_DIM), eye])
    n_qk_tiles = n_qk_cols // tn
    return pl.pallas_call(
        _pair_weight_kernel,
        grid=(layers, n_out // tn),
        in_specs=[
            pl.BlockSpec((None, d, tn), lambda l, j: (l, 0, j)),
            pl.BlockSpec((None, tn, tn), lambda l, j: (jnp.where(j < n_qk_tiles, 0, 1), 0, 0)),
        ],
        out_specs=pl.BlockSpec((None, d, tn), lambda l, j: (l, 0, j)),
        out_shape=jax.ShapeDtypeStruct((layers, d, n_out), BF16),
        compiler_params=_params("arbitrary", "arbitrary"),
        name="pair_qkv_weight",
    )(w_qkv, perms)


def _qkv_kernel(*refs, rope, side, row, tn, n_q_tiles, n_qk_tiles):
    refs = list(refs)
    tab_ref = refs.pop()
    if side:
        side_out_ref = refs.pop()
        side_in_ref = refs.pop(-2)
        side_out_ref[...] = side_in_ref[...].astype(BF16)
    x_ref, sh_ref, sc_ref = refs[:3]
    if rope:
        w_ref, gain_ref, cos_ref, sin_ref, o_ref = refs[3:]
    else:
        w_ref, gain_ref, o_ref = refs[3:]
    h = _modulated(x_ref[...], sh_ref, sc_ref, row)
    n_tiles = w_ref.shape[1] // tn
    gain_q = gain_ref[0:1, :] * SCORE_SCALE2
    gain_k = gain_ref[1:2, :]
    if rope:
        for t, gain in enumerate((gain_q, gain_k)):
            partner = pltpu.roll(jnp.broadcast_to(gain, (F32_ROWS, HEAD_DIM)), HEAD_DIM // 2, 1)[0:1, :]
            tab_ref[2 * t] = cos_ref[...] * gain
            tab_ref[2 * t + 1] = sin_ref[...] * partner
    y_next = _dot(h, w_ref[:, 0:tn])
    for jt in range(n_tiles):
        y, y_next = y_next, (_dot(h, w_ref[:, (jt + 1) * tn:(jt + 2) * tn]) if jt + 1 < n_tiles else None)
        if jt >= n_qk_tiles:
            o_ref[:, jt * tn:(jt + 1) * tn] = y.astype(BF16)
            continue
        is_q = jt < n_q_tiles
        for hh in range(tn // HEAD_DIM):
            yh = y[:, hh * HEAD_DIM:(hh + 1) * HEAD_DIM]
            r = lax.rsqrt(jnp.mean(yh * yh, axis=-1, keepdims=True) + EPS)
            if rope:
                t = 0 if is_q else 1
                yh = (yh * tab_ref[2 * t] + pltpu.roll(yh, HEAD_DIM // 2, 1) * tab_ref[2 * t + 1]) * r
            else:
                yh = (yh * r) * (gain_q if is_q else gain_k)
            o_ref[:, jt * tn + hh * HEAD_DIM:jt * tn + (hh + 1) * HEAD_DIM] = yh.astype(BF16)


def _qkv_proj(x, mod, layer, row, w_qkv_paired, jl, gains_paired, rope_tabs, tm, side_w=None, tn=512):
    m, d = x.shape
    n_out = w_qkv_paired.shape[2]
    d_q = d
    d_kv = (n_out - d_q) // 2
    rope = rope_tabs is not None
    in_specs = [
        pl.BlockSpec((tm, d), lambda i: (i, 0)),
        _mod_spec(layer, SHIFT_MIX, d),
        _mod_spec(layer, SCALE_MIX, d),
        pl.BlockSpec((None, d, n_out), lambda i: (jl, 0, 0), pipeline_mode=pl.Buffered(1)),
        pl.BlockSpec((2, HEAD_DIM), lambda i: (0, 0)),
    ]
    args = [x, mod, mod, w_qkv_paired, gains_paired]
    if rope:
        in_specs += [pl.BlockSpec((tm, HEAD_DIM), lambda i: (i, 0))] * 2
        args += list(rope_tabs)
    out_specs = [pl.BlockSpec((tm, n_out), lambda i: (i, 0))]
    out_shape = [jax.ShapeDtypeStruct((m, n_out), BF16)]
    if side_w is not None:
        side = _SideCast(side_w, jl, m // tm, lambda i: i)
        in_specs.append(side.in_spec)
        args.append(side.arg)
        out_specs.append(side.out_spec)
        out_shape.append(side.out_shape)
    kern = functools.partial(_qkv_kernel, rope=rope, side=side_w is not None, row=row, tn=tn,
                             n_q_tiles=d_q // tn, n_qk_tiles=(d_q + d_kv) // tn)
    outs = pl.pallas_call(
        kern,
        grid=(m // tm,),
        in_specs=in_specs,
        out_specs=out_specs,
        out_shape=out_shape,
        scratch_shapes=[pltpu.VMEM((4, tm, HEAD_DIM), F32)],
        compiler_params=_params("arbitrary"),
        name="qkv_proj",
    )(*args)
    return (outs[0], outs[1]) if side_w is not None else (outs[0], None)


def _sink_attend(t, v, sink2):
    rows, n_keys = t.shape
    m2 = jnp.maximum(jnp.broadcast_to(jnp.max(t, axis=-1, keepdims=True), (rows, HEAD_DIM)), sink2)
    e = jnp.exp2(t - jnp.tile(m2, (1, n_keys // HEAD_DIM))).astype(BF16)
    ov = _dot(e, jnp.concatenate([v, jnp.ones_like(v)], axis=1))
    return ov[:, :HEAD_DIM] / (ov[:, HEAD_DIM:] + jnp.exp2(sink2 - m2))


def _scores2(q, k):
    return lax.dot_general(q, k, (((1,), (1,)), ((), ())), preferred_element_type=F32)


def _band_attn_kernel(sink_ref, q_ref, kp_ref, km_ref, kn_ref, kc_ref, vp_ref, vm_ref, vn_ref, vc_ref, o_ref,
                      kw_ref, vw_ref):
    h = pl.program_id(0)
    i = pl.program_id(1)
    tq = q_ref.shape[0]
    nb = tq // WINDOW
    last_blk = pl.num_programs(1) * nb - 1
    kw_ref[0:WINDOW, :] = kp_ref[...]
    kw_ref[WINDOW:WINDOW + tq, :] = km_ref[...]
    kw_ref[WINDOW + tq:, :] = kn_ref[...]
    vw_ref[0:WINDOW, :] = vp_ref[...]
    vw_ref[WINDOW:WINDOW + tq, :] = vm_ref[...]
    vw_ref[WINDOW + tq:, :] = vn_ref[...]
    rows = GROUP * WINDOW
    r = lax.broadcasted_iota(jnp.int32, (rows, WINDOW), 0) & (WINDOW - 1)
    c_minus_r = lax.broadcasted_iota(jnp.int32, (rows, WINDOW), 1) - r
    sink2 = jnp.concatenate([jnp.full((WINDOW, HEAD_DIM), sink_ref[h * GROUP + g] * LOG2E, F32)
                             for g in range(GROUP)], axis=0)
    kc = kc_ref[...]
    vc = vc_ref[...]

    def masked_scores(b):
        blk = i * nb + b
        q4 = jnp.concatenate([q_ref[b * WINDOW:(b + 1) * WINDOW, g * HEAD_DIM:(g + 1) * HEAD_DIM]
                              for g in range(GROUP)], axis=0)
        s = _scores2(q4, jnp.concatenate([kw_ref[b * WINDOW:(b + 3) * WINDOW, :], kc], axis=0))
        lo = jnp.where(blk > 0, 0, WINDOW)
        hi = jnp.where(blk < last_blk, 0, -WINDOW)
        s_prev = jnp.where(c_minus_r >= lo, s[:, 0:WINDOW], NEG_INF)
        s_next = jnp.where(c_minus_r <= hi, s[:, 2 * WINDOW:3 * WINDOW], NEG_INF)
        return jnp.concatenate([s_prev, s[:, WINDOW:2 * WINDOW], s_next, s[:, 3 * WINDOW:]], axis=1)

    ahead = 1
    scores = {b: masked_scores(b) for b in range(min(ahead, nb))}
    for b in range(nb):
        if b + ahead < nb:
            scores[b + ahead] = masked_scores(b + ahead)
        v = jnp.concatenate([vw_ref[b * WINDOW:(b + 3) * WINDOW, :], vc], axis=0)
        o = _sink_attend(scores.pop(b), v, sink2)
        for g in range(GROUP):
            o_ref[b * WINDOW:(b + 1) * WINDOW, g * HEAD_DIM:(g + 1) * HEAD_DIM] = (
                o[g * WINDOW:(g + 1) * WINDOW, :].astype(BF16))


def _band_attention(qkv, qkv_c, sink, tq):
    n = qkv.shape[0]
    l_ctx = qkv_c.shape[0]
    d_q = N_KV_HEADS * GROUP * HEAD_DIM
    kcol = d_q // HEAD_DIM
    vcol = kcol + N_KV_HEADS
    per = tq // WINDOW
    nblk = n // WINDOW

    def prev(col0):
        return pl.BlockSpec((WINDOW, HEAD_DIM), lambda h, i: (jnp.maximum(i * per - 1, 0), col0 + h))

    def main(col0):
        return pl.BlockSpec((tq, HEAD_DIM), lambda h, i: (i, col0 + h))

    def nxt(col0):
        return pl.BlockSpec((WINDOW, HEAD_DIM), lambda h, i: (jnp.minimum((i + 1) * per, nblk - 1), col0 + h))

    def ctx(col0):
        return pl.BlockSpec((l_ctx, HEAD_DIM), lambda h, i: (0, col0 + h))

    return pl.pallas_call(
        _band_attn_kernel,
        grid=(N_KV_HEADS, n // tq),
        in_specs=[
            pl.BlockSpec(memory_space=pltpu.SMEM),
            pl.BlockSpec((tq, GROUP * HEAD_DIM), lambda h, i: (i, h)),
            prev(kcol), main(kcol), nxt(kcol), ctx(kcol),
            prev(vcol), main(vcol), nxt(vcol), ctx(vcol),
        ],
        out_specs=pl.BlockSpec((tq, GROUP * HEAD_DIM), lambda h, i: (i, h)),
        out_shape=jax.ShapeDtypeStruct((n, d_q), BF16),
        scratch_shapes=[pltpu.VMEM((tq + 2 * WINDOW, HEAD_DIM), BF16)] * 2,
        compiler_params=_params("arbitrary", "arbitrary"),
        name="band_attention",
    )(sink, qkv, qkv, qkv, qkv, qkv_c, qkv, qkv, qkv, qkv_c)


def _ctx_attn_kernel(sink_ref, q_ref, k_ref, v_ref, o_ref):
    h = pl.program_id(0)
    for g in range(GROUP):
        sl = slice(g * HEAD_DIM, (g + 1) * HEAD_DIM)
        sink2 = jnp.full((q_ref.shape[0], HEAD_DIM), sink_ref[h * GROUP + g] * LOG2E, F32)
        o_ref[:, sl] = _sink_attend(_scores2(q_ref[:, sl], k_ref[...]), v_ref[...], sink2).astype(BF16)


def _ctx_attention(qkv_c, sink):
    l_ctx = qkv_c.shape[0]
    d_q = N_KV_HEADS * GROUP * HEAD_DIM
    kcol = d_q // HEAD_DIM
    vcol = kcol + N_KV_HEADS
    return pl.pallas_call(
        _ctx_attn_kernel,
        grid=(N_KV_HEADS,),
        in_specs=[
            pl.BlockSpec(memory_space=pltpu.SMEM),
            pl.BlockSpec((l_ctx, GROUP * HEAD_DIM), lambda h: (0, h)),
            pl.BlockSpec((l_ctx, HEAD_DIM), lambda h: (0, kcol + h)),
            pl.BlockSpec((l_ctx, HEAD_DIM), lambda h: (0, vcol + h)),
        ],
        out_specs=pl.BlockSpec((l_ctx, GROUP * HEAD_DIM), lambda h: (0, h)),
        out_shape=jax.ShapeDtypeStruct((l_ctx, d_q), BF16),
        compiler_params=_params("arbitrary"),
        name="ctx_attention",
    )(sink, qkv_c, qkv_c, qkv_c)


def _conv_in_kernel(h_ref, hp_ref, hn_ref, wb_ref, wc_ref, wv_ref, cw_ref, *rest):
    if len(rest) == 4:
        side_in_ref, u_ref, side_out_ref, hs_ref = rest
        side_out_ref[...] = side_in_ref[...].astype(BF16)
    else:
        u_ref, hs_ref = rest

    @pl.when(pl.program_id(1) == 0)
    def _():
        _fill_halo_lhs(h_ref, hp_ref, hn_ref, hs_ref)

    wb = wb_ref[...].astype(BF16)
    wc = wc_ref[...].astype(BF16)
    wv = wv_ref[...].astype(BF16)
    def finish(c0, rc, p_conv):
        gate_b = _dot(hs_ref[c0 + HALO_ROWS:c0 + HALO_ROWS + rc, :], wb)
        u_ref[c0:c0 + rc, :] = (gate_b * p_conv).astype(BF16)

    _chunked_conv3(hs_ref, h_ref.shape[0], lambda lhs: _dot(lhs, wc) * _dot(lhs, wv), cw_ref, finish)


def _conv_in(h, w_in, jl, conv_w, tm, side_w=None, tn=256):
    m, d = h.shape
    nt = d // tn

    def wspec(part):
        return pl.BlockSpec((None, d, tn), lambda i, j: (jl, 0, part * nt + j))

    in_specs = _halo_lhs_specs(tm, d, m) + [wspec(0), wspec(1), wspec(2),
                                            pl.BlockSpec((None, 3, tn), lambda i, j: (jl, 0, j))]
    args = [h, h, h, w_in, w_in, w_in, conv_w]
    out_specs = [pl.BlockSpec((tm, tn), lambda i, j: (i, j))]
    out_shape = [jax.ShapeDtypeStruct((m, d), BF16)]
    if side_w is not None:
        side = _SideCast(side_w, jl, (m // tm) * nt, lambda i, j: i * nt + j)
        in_specs.append(side.in_spec)
        args.append(side.arg)
        out_specs.append(side.out_spec)
        out_shape.append(side.out_shape)
    outs = pl.pallas_call(
        _conv_in_kernel,
        grid=(m // tm, nt),
        in_specs=in_specs,
        out_specs=out_specs,
        out_shape=out_shape,
        scratch_shapes=[pltpu.VMEM((tm + 2 * HALO_ROWS, d), BF16)],
        compiler_params=_params("arbitrary", "arbitrary"),
        name="conv_in",
    )(*args)
    return (outs[0], outs[1]) if side_w is not None else (outs[0], None)


def _ffn_up_kernel(h_ref, hp_ref, hn_ref, wg_ref, wv_ref, cw_ref, cb_ref, *rest):
    if len(rest) == 4:
        side_in_ref, a_ref, side_out_ref, hs_ref = rest
        side_out_ref[...] = side_in_ref[...].astype(BF16)
    else:
        a_ref, hs_ref = rest

    @pl.when(pl.program_id(1) == 0)
    def _():
        _fill_halo_lhs(h_ref, hp_ref, hn_ref, hs_ref)

    wg = wg_ref[...].astype(BF16)
    wv = wv_ref[...].astype(BF16)

    def finish(c0, rc, gate_conv):
        val = _dot(hs_ref[c0 + HALO_ROWS:c0 + HALO_ROWS + rc, :], wv)
        a_ref[c0:c0 + rc, :] = (jax.nn.silu(gate_conv + cb_ref[...]) * val).astype(BF16)

    _chunked_conv3(hs_ref, h_ref.shape[0], lambda lhs: _dot(lhs, wg), cw_ref, finish)


def _ffn_up(h, w_up, layer, conv_w, conv_b, tm, side_w=None, tn=256):
    m, d = h.shape
    f = w_up.shape[2] // 2
    nt = f // tn
    in_specs = _halo_lhs_specs(tm, d, m) + [
        pl.BlockSpec((None, d, tn), lambda i, j: (layer, 0, j)),
        pl.BlockSpec((None, d, tn), lambda i, j: (layer, 0, nt + j)),
        pl.BlockSpec((None, 3, tn), lambda i, j: (layer, 0, j)),
        pl.BlockSpec((None, 1, tn), lambda i, j: (layer, 0, j)),
    ]
    args = [h, h, h, w_up, w_up, conv_w, conv_b.reshape(conv_b.shape[0], 1, f)]
    out_specs = [pl.BlockSpec((tm, tn), lambda i, j: (i, j))]
    out_shape = [jax.ShapeDtypeStruct((m, f), BF16)]
    if side_w is not None:
        side = _SideCast(side_w, layer, (m // tm) * nt, lambda i, j: i * nt + j)
        in_specs.append(side.in_spec)
        args.append(side.arg)
        out_specs.append(side.out_spec)
        out_shape.append(side.out_shape)
    outs = pl.pallas_call(
        _ffn_up_kernel,
        grid=(m // tm, nt),
        in_specs=in_specs,
        out_specs=out_specs,
        out_shape=out_shape,
        scratch_shapes=[pltpu.VMEM((tm + 2 * HALO_ROWS, d), BF16)],
        compiler_params=_params("arbitrary", "arbitrary"),
        name="ffn_up",
    )(*args)
    return (outs[0], outs[1]) if side_w is not None else (outs[0], None)


def _out_proj_kernel(*refs, row, has_next, n_k):
    if has_next:
        a_ref, w_ref, x_ref, g_ref, sh_ref, sc_ref, o_ref, hn_ref = refs
    else:
        a_ref, w_ref, x_ref, g_ref, o_ref = refs
    k = pl.program_id(1)
    last = n_k - 1
    tm = a_ref.shape[0]
    rc = min(OUT_ROW_CHUNK, tm)
    chunks = [slice(c0, c0 + rc) for c0 in range(0, tm, rc)]

    def finish(rows, acc):
        xn = x_ref[rows, :] + g_ref[row:row + 1, :] * acc
        o_ref[rows, :] = xn
        if has_next:
            hn_ref[rows, :] = _modulated(xn, sh_ref, sc_ref, row)

    if n_k == 1:
        for rows in chunks:
            finish(rows, _dot(a_ref[rows, :], w_ref[...]))
        return

    @pl.when(k == 0)
    def _():
        for rows in chunks:
            o_ref[rows, :] = _dot(a_ref[rows, :], w_ref[...])

    @pl.when((k > 0) & (k < last))
    def _():
        for rows in chunks:
            o_ref[rows, :] += _dot(a_ref[rows, :], w_ref[...])

    @pl.when(k == last)
    def _():
        part_next = _dot(a_ref[chunks[0], :], w_ref[...])
        for ci, rows in enumerate(chunks):
            part, part_next = part_next, (_dot(a_ref[chunks[ci + 1], :], w_ref[...])
                                          if ci + 1 < len(chunks) else None)
            finish(rows, o_ref[rows, :] + part)


def _out_proj(a, w_bf16, jl, x, mod, layer, gate_chunk, nxt, row, tm, tk=512):
    m, d = x.shape
    kdim = a.shape[1]
    in_specs = [
        pl.BlockSpec((tm, tk), lambda i, k: (i, k)),
        pl.BlockSpec((None, tk, d), lambda i, k: (jl, k, 0)),
        pl.BlockSpec((tm, d), lambda i, k: (i, 0)),
        _mod_spec(layer, gate_chunk, d),
    ]
    args = [a, w_bf16, x, mod]
    out_specs = [pl.BlockSpec((tm, d), lambda i, k: (i, 0))]
    out_shape = [jax.ShapeDtypeStruct((m, d), F32)]
    if nxt is not None:
        in_specs += [_mod_spec(nxt[0], nxt[1], d), _mod_spec(nxt[0], nxt[2], d)]
        args += [mod, mod]
        out_specs.append(pl.BlockSpec((tm, d), lambda i, k: (i, 0)))
        out_shape.append(jax.ShapeDtypeStruct((m, d), BF16))
    outs = pl.pallas_call(
        functools.partial(_out_proj_kernel, row=row, has_next=nxt is not None, n_k=kdim // tk),
        grid=(m // tm, kdim // tk),
        in_specs=in_specs,
        out_specs=out_specs,
        out_shape=out_shape,
        compiler_params=_params("arbitrary", "arbitrary"),
        name="out_proj",
    )(*args)
    return (outs[0], outs[1]) if nxt is not None else (outs[0], None)


def _out_proj_resident_kernel(*refs, row, has_next):
    if has_next:
        a_ref, w_ref, x_ref, g_ref, sh_ref, sc_ref, o_ref, hn_ref = refs
    else:
        a_ref, w_ref, x_ref, g_ref, o_ref = refs
    tm = a_ref.shape[0]
    rc = min(OUT_ROW_CHUNK, tm)
    chunks = [slice(c0, c0 + rc) for c0 in range(0, tm, rc)]
    acc_next = _dot(a_ref[chunks[0], :], w_ref[...])
    for ci, rows in enumerate(chunks):
        acc, acc_next = acc_next, (_dot(a_ref[chunks[ci + 1], :], w_ref[...]) if ci + 1 < len(chunks) else None)
        xn = x_ref[rows, :] + g_ref[row:row + 1, :] * acc
        o_ref[rows, :] = xn
        if has_next:
            hn_ref[rows, :] = _modulated(xn, sh_ref, sc_ref, row)


def _out_proj_resident(a, w_bf16, jl, x, mod, layer, gate_chunk, nxt, row, tm):
    m, d = x.shape
    kdim = a.shape[1]
    in_specs = [
        pl.BlockSpec((tm, kdim), lambda i: (i, 0)),
        pl.BlockSpec((None, kdim, d), lambda i: (jl, 0, 0), pipeline_mode=pl.Buffered(1)),
        pl.BlockSpec((tm, d), lambda i: (i, 0)),
        _mod_spec(layer, gate_chunk, d),
    ]
    args = [a, w_bf16, x, mod]
    out_specs = [pl.BlockSpec((tm, d), lambda i: (i, 0))]
    out_shape = [jax.ShapeDtypeStruct((m, d), F32)]
    if nxt is not None:
        in_specs += [_mod_spec(nxt[0], nxt[1], d), _mod_spec(nxt[0], nxt[2], d)]
        args += [mod, mod]
        out_specs.append(pl.BlockSpec((tm, d), lambda i: (i, 0)))
        out_shape.append(jax.ShapeDtypeStruct((m, d), BF16))
    outs = pl.pallas_call(
        functools.partial(_out_proj_resident_kernel, row=row, has_next=nxt is not None),
        grid=(m // tm,),
        in_specs=in_specs,
        out_specs=out_specs,
        out_shape=out_shape,
        compiler_params=_params("arbitrary"),
        name="out_proj_resident",
    )(*args)
    return (outs[0], outs[1]) if nxt is not None else (outs[0], None)


def _rope_tables(n):
    rows = n // GRID_W
    inv = ROPE_BASE ** (-jnp.arange(ROPE_PAIRS, dtype=F32) / ROPE_PAIRS)
    row_ang = jnp.arange(rows, dtype=F32)[:, None] * inv
    col_ang = jnp.arange(GRID_W, dtype=F32)[:, None] * inv

    def per_token(row_tab, col_tab, lo_sign):
        r = jnp.repeat(row_tab, GRID_W, axis=0)
        c = jnp.tile(col_tab, (rows, 1))
        return jnp.concatenate([lo_sign * r, lo_sign * c, r, c], axis=1)

    return per_token(jnp.cos(row_ang), jnp.cos(col_ang), 1.0), per_token(jnp.sin(row_ang), jnp.sin(col_ang), -1.0)


def kernel(x, c, ctx, c_ctx, w_ada, b_ada, attn_w_qkv, attn_w_o, attn_q_gain, attn_k_gain, attn_sink,
           sc_w_in, sc_conv, sc_w_out, ffn_w_up, ffn_conv, ffn_conv_b, ffn_w_down):
    batch, n, d = x.shape
    l_ctx = ctx.shape[1]
    depth = w_ada.shape[0]
    assert batch == 1 and d == N_KV_HEADS * GROUP * HEAD_DIM
    tm_in = min(2048, n)
    tm_out = min(1024, n)
    tm_res = min(512, n)
    tq = min(2048, n)
    tm_c = l_ctx

    cc = jnp.concatenate([c, c_ctx[None, :], jnp.zeros((MOD_ROWS - 2, d), F32)], axis=0)
    mod = _ada_table(cc, w_ada, b_ada)
    rope_tabs = _rope_tables(n)
    n_qk_heads = N_KV_HEADS * GROUP + N_KV_HEADS
    w_qkv = _paired_qkv_weight(attn_w_qkv, n_qk_heads * HEAD_DIM)
    xs, cs = x[0], ctx[0]
    hx = hc = None

    for l in range(depth):
        is_attn = (l % N_MIXERS) == 0
        j = l // N_MIXERS
        need_ctx = l < depth - 1
        nxt_ffn = (l, SHIFT_FFN, SCALE_FFN)
        next_is_conv = l + 1 < depth and ((l + 1) % N_MIXERS) != 0
        nxt_mix = (l + 1, SHIFT_MIX, SCALE_MIX) if next_is_conv else None
        if is_attn:
            gains = _paired_lanes(jnp.stack([attn_q_gain[j], attn_k_gain[j]]), 1)
            qkv, w_o = _qkv_proj(xs, mod, l, ROW_LATENT, w_qkv, j, gains, rope_tabs, tm_res, side_w=attn_w_o)
            qkv_c, _ = _qkv_proj(cs, mod, l, ROW_CTX, w_qkv, j, gains, None, tm_c)
            o = _band_attention(qkv, qkv_c, attn_sink[j], tq)
            xs, hx = _out_proj_resident(o, w_o[None], 0, xs, mod, l, GATE_MIX, nxt_ffn, ROW_LATENT, tm_res)
            if need_ctx:
                o_c = _ctx_attention(qkv_c, attn_sink[j])
                cs, hc = _out_proj_resident(o_c, w_o[None], 0, cs, mod, l, GATE_MIX, nxt_ffn, ROW_CTX, tm_c)
        else:
            u, w_out = _conv_in(hx, sc_w_in, j, sc_conv, tm_in, side_w=sc_w_out)
            xs, hx = _out_proj_resident(u, w_out[None], 0, xs, mod, l, GATE_MIX, nxt_ffn, ROW_LATENT, tm_res)
            if need_ctx:
                u_c, _ = _conv_in(hc, sc_w_in, j, sc_conv, tm_c, tn=CTX_TILE)
                cs, hc = _out_proj_resident(u_c, w_out[None], 0, cs, mod, l, GATE_MIX, nxt_ffn, ROW_CTX, tm_c)
        a, w_down = _ffn_up(hx, ffn_w_up, l, ffn_conv, ffn_conv_b, tm_in, side_w=ffn_w_down)
        xs, hx = _out_proj(a, w_down[None], 0, xs, mod, l, GATE_FFN, nxt_mix, ROW_LATENT, tm_out)
        if need_ctx:
            a_c, _ = _ffn_up(hc, ffn_w_up, l, ffn_conv, ffn_conv_b, tm_c, tn=CTX_TILE)
            f_half = a_c.shape[1] // 2
            tk_c = f_half if f_half % HEAD_DIM == 0 else CTX_TILE
            cs, hc = _out_proj(a_c, w_down[None], 0, cs, mod, l, GATE_FFN, nxt_mix, ROW_CTX, tm_c, tk=tk_c)
    return xs[None]
```

```python
import functools

import jax
import jax.numpy as jnp
from jax import lax
from jax.experimental import pallas as pl
from jax.experimental.pallas import tpu as pltpu

HEAD_DIM = 128
N_KV_HEADS = 4
GROUP = 4
WINDOW = 128
GRID_W = 64
ROPE_PAIRS = HEAD_DIM // 4
ROPE_BASE = 10000.0
N_MIXERS = 2
EPS = 1e-6
NEG_INF = -1e30

BF16 = jnp.bfloat16
F32 = jnp.float32

V7X_VMEM_LIMIT_BYTES = 56 * 1024 * 1024
HALO_ROWS = 16
F32_ROWS = 8
MOD_ROWS = 8
ROW_LATENT = 0
ROW_CTX = 1
ROW_CHUNK = 512
CTX_TILE = 512
OUT_ROW_CHUNK = 256
ROW_TILE_STREAMED = 2048
ROW_TILE_OUT = 1024
ROW_TILE_RESIDENT = 512
QUERY_TILE = 2048
PROJ_COL_TILE = 256
QKV_COL_TILE = 512
OUT_K_TILE = 512
ADA_COL_TILE = 1024
LOG2E = 1.4426950408889634
SCORE_SCALE2 = HEAD_DIM ** -0.5 * LOG2E
SHIFT_MIX, SCALE_MIX, GATE_MIX, SHIFT_FFN, SCALE_FFN, GATE_FFN = range(6)


def _params(*sem):
    return pltpu.CompilerParams(dimension_semantics=sem, vmem_limit_bytes=V7X_VMEM_LIMIT_BYTES)


def _dot(a, b):
    return jnp.dot(a, b, preferred_element_type=F32)


def _ada_kernel(cc_ref, w_ref, b_ref, o_ref):
    a = jax.nn.silu(cc_ref[...]).astype(BF16)
    o_ref[...] = _dot(a, w_ref[...].astype(BF16)) + b_ref[...]


def _ada_table(cc, w_ada, b_ada, tn=ADA_COL_TILE):
    depth, d, d6 = w_ada.shape
    return pl.pallas_call(
        _ada_kernel,
        grid=(depth, d6 // tn),
        in_specs=[
            pl.BlockSpec((MOD_ROWS, d), lambda l, j: (0, 0)),
            pl.BlockSpec((None, d, tn), lambda l, j: (l, 0, j)),
            pl.BlockSpec((None, 1, tn), lambda l, j: (l, 0, j)),
        ],
        out_specs=pl.BlockSpec((None, MOD_ROWS, tn), lambda l, j: (l, 0, j)),
        out_shape=jax.ShapeDtypeStruct((depth, MOD_ROWS, d6), F32),
        compiler_params=_params("arbitrary", "arbitrary"),
        name="ada_table",
    )(cc, w_ada, b_ada.reshape(depth, 1, d6))


def _modulated(x, sh_ref, sc_ref, row):
    r = lax.rsqrt(jnp.mean(x * x, axis=-1, keepdims=True) + EPS)
    return ((x * r) * (1 + sc_ref[row:row + 1, :]) + sh_ref[row:row + 1, :]).astype(BF16)


def _mod_spec(layer, chunk, d):
    return pl.BlockSpec((None, MOD_ROWS, d), lambda *_: (layer, 0, chunk))


def _fill_halo_lhs(h_ref, hp_ref, hn_ref, hs_ref):
    i = pl.program_id(0)
    tm = h_ref.shape[0]
    zero = jnp.zeros(hp_ref.shape, BF16)
    hs_ref[0:HALO_ROWS, :] = jnp.where(i > 0, hp_ref[...], zero)
    hs_ref[HALO_ROWS:HALO_ROWS + tm, :] = h_ref[...]
    hs_ref[HALO_ROWS + tm:, :] = jnp.where(i < pl.num_programs(0) - 1, hn_ref[...], zero)


def _halo_lhs_specs(tm, d, m):
    per = tm // HALO_ROWS
    nblk = m // HALO_ROWS
    return [
        pl.BlockSpec((tm, d), lambda i, j: (i, 0)),
        pl.BlockSpec((HALO_ROWS, d), lambda i, j: (jnp.maximum(i * per - 1, 0), 0)),
        pl.BlockSpec((HALO_ROWS, d), lambda i, j: (jnp.minimum((i + 1) * per, nblk - 1), 0)),
    ]


class _SideCast:
    def __init__(self, w, layer, n_steps, step_of):
        _, kdim, d = w.shape
        rows = kdim // n_steps
        assert kdim % n_steps == 0 and rows % HALO_ROWS == 0
        self.arg = w
        self.in_spec = pl.BlockSpec((None, rows, d), lambda *g: (layer, step_of(*g), 0))
        self.out_spec = pl.BlockSpec((rows, d), lambda *g: (step_of(*g), 0))
        self.out_shape = jax.ShapeDtypeStruct((kdim, d), BF16)


def _row_chunks(tm):
    rc = min(ROW_CHUNK, tm)
    return [(c0, rc) for c0 in range(0, tm, rc)]


def _chunked_conv3(hs_ref, tm, project, w_ref, finish):
    chunks = _row_chunks(tm)
    last = len(chunks) - 1
    ys = []

    def body(ci):
        off = HALO_ROWS if ci == 0 else 0
        return ys[ci][off:off + chunks[ci][1]]

    def emit(ci):
        c0, rc = chunks[ci]
        before = ys[0][HALO_ROWS - F32_ROWS:HALO_ROWS] if ci == 0 else body(ci - 1)[-F32_ROWS:]
        if ci == last:
            off = HALO_ROWS if ci == 0 else 0
            after = ys[ci][off + rc:off + rc + F32_ROWS]
        else:
            after = body(ci + 1)[:F32_ROWS]
        ext = jnp.concatenate([before, body(ci), after], axis=0)
        sl = slice(F32_ROWS, F32_ROWS + rc)
        dn = pltpu.roll(ext, 1, 0)[sl]
        up = pltpu.roll(ext, ext.shape[0] - 1, 0)[sl]
        finish(c0, rc, dn * w_ref[0:1, :] + ext[sl] * w_ref[1:2, :] + up * w_ref[2:3, :])

    for ci, (c0, rc) in enumerate(chunks):
        lo = c0 + (0 if ci == 0 else HALO_ROWS)
        hi = c0 + rc + HALO_ROWS + (HALO_ROWS if ci == last else 0)
        ys.append(project(hs_ref[lo:hi, :]))
        if ci >= 1:
            emit(ci - 1)
    emit(last)


def _paired_lanes(a, n_heads):
    lead = a.shape[:-1]
    return a.reshape(*lead, n_heads, 2, 2, ROPE_PAIRS).swapaxes(-3, -2).reshape(*lead, n_heads * HEAD_DIM)


def _pair_weight_kernel(w_ref, p_ref, o_ref):
    o_ref[...] = _dot(w_ref[...].astype(BF16), p_ref[...]).astype(BF16)


def _paired_qkv_weight(w_qkv, n_qk_cols, tn=QKV_COL_TILE):
    layers, d, n_out = w_qkv.shape
    eye = jnp.eye(tn, dtype=BF16)
    perms = jnp.stack([_paired_lanes(eye, tn // HEAD_DIM), eye])
    n_qk_tiles = n_qk_cols // tn
    return pl.pallas_call(
        _pair_weight_kernel,
        grid=(layers, n_out // tn),
        in_specs=[
            pl.BlockSpec((None, d, tn), lambda l, j: (l, 0, j)),
            pl.BlockSpec((None, tn, tn), lambda l, j: (jnp.where(j < n_qk_tiles, 0, 1), 0, 0)),
        ],
        out_specs=pl.BlockSpec((None, d, tn), lambda l, j: (l, 0, j)),
        out_shape=jax.ShapeDtypeStruct((layers, d, n_out), BF16),
        compiler_params=_params("arbitrary", "arbitrary"),
        name="pair_qkv_weight",
    )(w_qkv, perms)


def _qkv_kernel(*refs, rope, side, row, tn, n_q_tiles, n_qk_tiles):
    refs = list(refs)
    if rope:
        tab_ref = refs.pop()
    if side:
        side_out_ref = refs.pop()
        side_in_ref = refs.pop(-2)
        side_out_ref[...] = side_in_ref[...].astype(BF16)
    x_ref, sh_ref, sc_ref = refs[:3]
    if rope:
        w_ref, gain_ref, cos_ref, sin_ref, o_ref = refs[3:]
    else:
        w_ref, gain_ref, o_ref = refs[3:]
    h = _modulated(x_ref[...], sh_ref, sc_ref, row)
    n_tiles = w_ref.shape[1] // tn
    gain_q = gain_ref[0:1, :] * SCORE_SCALE2
    gain_k = gain_ref[1:2, :]
    if rope:
        for t, gain in enumerate((gain_q, gain_k)):
            partner = pltpu.roll(jnp.broadcast_to(gain, (F32_ROWS, HEAD_DIM)), HEAD_DIM // 2, 1)[0:1, :]
            tab_ref[2 * t] = cos_ref[...] * gain
            tab_ref[2 * t + 1] = sin_ref[...] * partner
    y_next = _dot(h, w_ref[:, 0:tn])
    for jt in range(n_tiles):
        y, y_next = y_next, (_dot(h, w_ref[:, (jt + 1) * tn:(jt + 2) * tn]) if jt + 1 < n_tiles else None)
        if jt >= n_qk_tiles:
            o_ref[:, jt * tn:(jt + 1) * tn] = y.astype(BF16)
            continue
        is_q = jt < n_q_tiles
        for hh in range(tn // HEAD_DIM):
            yh = y[:, hh * HEAD_DIM:(hh + 1) * HEAD_DIM]
            r = lax.rsqrt(jnp.mean(yh * yh, axis=-1, keepdims=True) + EPS)
            if rope:
                t = 0 if is_q else 1
                yh = (yh * tab_ref[2 * t] + pltpu.roll(yh, HEAD_DIM // 2, 1) * tab_ref[2 * t + 1]) * r
            else:
                yh = (yh * r) * (gain_q if is_q else gain_k)
            o_ref[:, jt * tn + hh * HEAD_DIM:jt * tn + (hh + 1) * HEAD_DIM] = yh.astype(BF16)


def _qkv_proj(x, mod, layer, row, w_qkv_paired, jl, gains_paired, rope_tabs, tm, side_w=None, tn=QKV_COL_TILE):
    m, d = x.shape
    n_out = w_qkv_paired.shape[2]
    d_q = d
    d_kv = (n_out - d_q) // 2
    rope = rope_tabs is not None
    in_specs = [
        pl.BlockSpec((tm, d), lambda i: (i, 0)),
        _mod_spec(layer, SHIFT_MIX, d),
        _mod_spec(layer, SCALE_MIX, d),
        pl.BlockSpec((None, d, n_out), lambda i: (jl, 0, 0), pipeline_mode=pl.Buffered(1)),
        pl.BlockSpec((2, HEAD_DIM), lambda i: (0, 0)),
    ]
    args = [x, mod, mod, w_qkv_paired, gains_paired]
    if rope:
        in_specs += [pl.BlockSpec((tm, HEAD_DIM), lambda i: (i, 0))] * 2
        args += list(rope_tabs)
    out_specs = [pl.BlockSpec((tm, n_out), lambda i: (i, 0))]
    out_shape = [jax.ShapeDtypeStruct((m, n_out), BF16)]
    if side_w is not None:
        side = _SideCast(side_w, jl, m // tm, lambda i: i)
        in_specs.append(side.in_spec)
        args.append(side.arg)
        out_specs.append(side.out_spec)
        out_shape.append(side.out_shape)
    kern = functools.partial(_qkv_kernel, rope=rope, side=side_w is not None, row=row, tn=tn,
                             n_q_tiles=d_q // tn, n_qk_tiles=(d_q + d_kv) // tn)
    outs = pl.pallas_call(
        kern,
        grid=(m // tm,),
        in_specs=in_specs,
        out_specs=out_specs,
        out_shape=out_shape,
        scratch_shapes=[pltpu.VMEM((4, tm, HEAD_DIM), F32)] if rope else [],
        compiler_params=_params("arbitrary"),
        name="qkv_proj",
    )(*args)
    return (outs[0], outs[1]) if side_w is not None else (outs[0], None)


def _sink_attend(t, v, sink2):
    rows, n_keys = t.shape
    m2 = jnp.maximum(jnp.broadcast_to(jnp.max(t, axis=-1, keepdims=True), (rows, HEAD_DIM)), sink2)
    e = jnp.exp2(t - jnp.tile(m2, (1, n_keys // HEAD_DIM))).astype(BF16)
    ov = _dot(e, jnp.concatenate([v, jnp.ones_like(v)], axis=1))
    return ov[:, :HEAD_DIM] / (ov[:, HEAD_DIM:] + jnp.exp2(sink2 - m2))


def _scores2(q, k):
    return lax.dot_general(q, k, (((1,), (1,)), ((), ())), preferred_element_type=F32)


def _band_attn_kernel(sink_ref, q_ref, kp_ref, km_ref, kn_ref, kc_ref, vp_ref, vm_ref, vn_ref, vc_ref, o_ref,
                      kw_ref, vw_ref):
    h = pl.program_id(0)
    i = pl.program_id(1)
    tq = q_ref.shape[0]
    nb = tq // WINDOW
    last_blk = pl.num_programs(1) * nb - 1
    kw_ref[0:WINDOW, :] = kp_ref[...]
    kw_ref[WINDOW:WINDOW + tq, :] = km_ref[...]
    kw_ref[WINDOW + tq:, :] = kn_ref[...]
    vw_ref[0:WINDOW, :] = vp_ref[...]
    vw_ref[WINDOW:WINDOW + tq, :] = vm_ref[...]
    vw_ref[WINDOW + tq:, :] = vn_ref[...]
    rows = GROUP * WINDOW
    r = lax.broadcasted_iota(jnp.int32, (rows, WINDOW), 0) & (WINDOW - 1)
    c_minus_r = lax.broadcasted_iota(jnp.int32, (rows, WINDOW), 1) - r
    sink2 = jnp.concatenate([jnp.full((WINDOW, HEAD_DIM), sink_ref[h * GROUP + g] * LOG2E, F32)
                             for g in range(GROUP)], axis=0)
    kc = kc_ref[...]
    vc = vc_ref[...]

    def masked_scores(b):
        blk = i * nb + b
        q4 = jnp.concatenate([q_ref[b * WINDOW:(b + 1) * WINDOW, g * HEAD_DIM:(g + 1) * HEAD_DIM]
                              for g in range(GROUP)], axis=0)
        s = _scores2(q4, jnp.concatenate([kw_ref[b * WINDOW:(b + 3) * WINDOW, :], kc], axis=0))
        lo = jnp.where(blk > 0, 0, WINDOW)
        hi = jnp.where(blk < last_blk, 0, -WINDOW)
        s_prev = jnp.where(c_minus_r >= lo, s[:, 0:WINDOW], NEG_INF)
        s_next = jnp.where(c_minus_r <= hi, s[:, 2 * WINDOW:3 * WINDOW], NEG_INF)
        return jnp.concatenate([s_prev, s[:, WINDOW:2 * WINDOW], s_next, s[:, 3 * WINDOW:]], axis=1)

    s_next = masked_scores(0)
    for b in range(nb):
        s_cur, s_next = s_next, (masked_scores(b + 1) if b + 1 < nb else None)
        v = jnp.concatenate([vw_ref[b * WINDOW:(b + 3) * WINDOW, :], vc], axis=0)
        o = _sink_attend(s_cur, v, sink2)
        for g in range(GROUP):
            o_ref[b * WINDOW:(b + 1) * WINDOW, g * HEAD_DIM:(g + 1) * HEAD_DIM] = (
                o[g * WINDOW:(g + 1) * WINDOW, :].astype(BF16))


def _band_attention(qkv, qkv_c, sink, tq):
    n = qkv.shape[0]
    l_ctx = qkv_c.shape[0]
    d_q = N_KV_HEADS * GROUP * HEAD_DIM
    kcol = d_q // HEAD_DIM
    vcol = kcol + N_KV_HEADS
    per = tq // WINDOW
    nblk = n // WINDOW

    def prev(col0):
        return pl.BlockSpec((WINDOW, HEAD_DIM), lambda h, i: (jnp.maximum(i * per - 1, 0), col0 + h))

    def main(col0):
        return pl.BlockSpec((tq, HEAD_DIM), lambda h, i: (i, col0 + h))

    def nxt(col0):
        return pl.BlockSpec((WINDOW, HEAD_DIM), lambda h, i: (jnp.minimum((i + 1) * per, nblk - 1), col0 + h))

    def ctx(col0):
        return pl.BlockSpec((l_ctx, HEAD_DIM), lambda h, i: (0, col0 + h))

    return pl.pallas_call(
        _band_attn_kernel,
        grid=(N_KV_HEADS, n // tq),
        in_specs=[
            pl.BlockSpec(memory_space=pltpu.SMEM),
            pl.BlockSpec((tq, GROUP * HEAD_DIM), lambda h, i: (i, h)),
            prev(kcol), main(kcol), nxt(kcol), ctx(kcol),
            prev(vcol), main(vcol), nxt(vcol), ctx(vcol),
        ],
        out_specs=pl.BlockSpec((tq, GROUP * HEAD_DIM), lambda h, i: (i, h)),
        out_shape=jax.ShapeDtypeStruct((n, d_q), BF16),
        scratch_shapes=[pltpu.VMEM((tq + 2 * WINDOW, HEAD_DIM), BF16)] * 2,
        compiler_params=_params("arbitrary", "arbitrary"),
        name="band_attention",
    )(sink, qkv, qkv, qkv, qkv, qkv_c, qkv, qkv, qkv, qkv_c)


def _ctx_attn_kernel(sink_ref, q_ref, k_ref, v_ref, o_ref):
    h = pl.program_id(0)
    for g in range(GROUP):
        sl = slice(g * HEAD_DIM, (g + 1) * HEAD_DIM)
        sink2 = jnp.full((q_ref.shape[0], HEAD_DIM), sink_ref[h * GROUP + g] * LOG2E, F32)
        o_ref[:, sl] = _sink_attend(_scores2(q_ref[:, sl], k_ref[...]), v_ref[...], sink2).astype(BF16)


def _ctx_attention(qkv_c, sink):
    l_ctx = qkv_c.shape[0]
    d_q = N_KV_HEADS * GROUP * HEAD_DIM
    kcol = d_q // HEAD_DIM
    vcol = kcol + N_KV_HEADS
    return pl.pallas_call(
        _ctx_attn_kernel,
        grid=(N_KV_HEADS,),
        in_specs=[
            pl.BlockSpec(memory_space=pltpu.SMEM),
            pl.BlockSpec((l_ctx, GROUP * HEAD_DIM), lambda h: (0, h)),
            pl.BlockSpec((l_ctx, HEAD_DIM), lambda h: (0, kcol + h)),
            pl.BlockSpec((l_ctx, HEAD_DIM), lambda h: (0, vcol + h)),
        ],
        out_specs=pl.BlockSpec((l_ctx, GROUP * HEAD_DIM), lambda h: (0, h)),
        out_shape=jax.ShapeDtypeStruct((l_ctx, d_q), BF16),
        compiler_params=_params("arbitrary"),
        name="ctx_attention",
    )(sink, qkv_c, qkv_c, qkv_c)


def _conv_in_kernel(h_ref, hp_ref, hn_ref, wb_ref, wc_ref, wv_ref, cw_ref, *rest):
    if len(rest) == 4:
        side_in_ref, u_ref, side_out_ref, hs_ref = rest
        side_out_ref[...] = side_in_ref[...].astype(BF16)
    else:
        u_ref, hs_ref = rest

    @pl.when(pl.program_id(1) == 0)
    def _():
        _fill_halo_lhs(h_ref, hp_ref, hn_ref, hs_ref)

    wb = wb_ref[...].astype(BF16)
    wc = wc_ref[...].astype(BF16)
    wv = wv_ref[...].astype(BF16)
    def finish(c0, rc, p_conv):
        gate_b = _dot(hs_ref[c0 + HALO_ROWS:c0 + HALO_ROWS + rc, :], wb)
        u_ref[c0:c0 + rc, :] = (gate_b * p_conv).astype(BF16)

    _chunked_conv3(hs_ref, h_ref.shape[0], lambda lhs: _dot(lhs, wc) * _dot(lhs, wv), cw_ref, finish)


def _conv_in(h, w_in, jl, conv_w, tm, side_w=None, tn=PROJ_COL_TILE):
    m, d = h.shape
    nt = d // tn

    def wspec(part):
        return pl.BlockSpec((None, d, tn), lambda i, j: (jl, 0, part * nt + j))

    in_specs = _halo_lhs_specs(tm, d, m) + [wspec(0), wspec(1), wspec(2),
                                            pl.BlockSpec((None, 3, tn), lambda i, j: (jl, 0, j))]
    args = [h, h, h, w_in, w_in, w_in, conv_w]
    out_specs = [pl.BlockSpec((tm, tn), lambda i, j: (i, j))]
    out_shape = [jax.ShapeDtypeStruct((m, d), BF16)]
    if side_w is not None:
        side = _SideCast(side_w, jl, (m // tm) * nt, lambda i, j: i * nt + j)
        in_specs.append(side.in_spec)
        args.append(side.arg)
        out_specs.append(side.out_spec)
        out_shape.append(side.out_shape)
    outs = pl.pallas_call(
        _conv_in_kernel,
        grid=(m // tm, nt),
        in_specs=in_specs,
        out_specs=out_specs,
        out_shape=out_shape,
        scratch_shapes=[pltpu.VMEM((tm + 2 * HALO_ROWS, d), BF16)],
        compiler_params=_params("arbitrary", "arbitrary"),
        name="conv_in",
    )(*args)
    return (outs[0], outs[1]) if side_w is not None else (outs[0], None)


def _ffn_up_kernel(h_ref, hp_ref, hn_ref, wg_ref, wv_ref, cw_ref, cb_ref, *rest):
    if len(rest) == 4:
        side_in_ref, a_ref, side_out_ref, hs_ref = rest
        side_out_ref[...] = side_in_ref[...].astype(BF16)
    else:
        a_ref, hs_ref = rest

    @pl.when(pl.program_id(1) == 0)
    def _():
        _fill_halo_lhs(h_ref, hp_ref, hn_ref, hs_ref)

    wg = wg_ref[...].astype(BF16)
    wv = wv_ref[...].astype(BF16)

    def finish(c0, rc, gate_conv):
        val = _dot(hs_ref[c0 + HALO_ROWS:c0 + HALO_ROWS + rc, :], wv)
        a_ref[c0:c0 + rc, :] = (jax.nn.silu(gate_conv + cb_ref[...]) * val).astype(BF16)

    _chunked_conv3(hs_ref, h_ref.shape[0], lambda lhs: _dot(lhs, wg), cw_ref, finish)


def _ffn_up(h, w_up, layer, conv_w, conv_b, tm, side_w=None, tn=PROJ_COL_TILE):
    m, d = h.shape
    f = w_up.shape[2] // 2
    nt = f // tn
    in_specs = _halo_lhs_specs(tm, d, m) + [
        pl.BlockSpec((None, d, tn), lambda i, j: (layer, 0, j)),
        pl.BlockSpec((None, d, tn), lambda i, j: (layer, 0, nt + j)),
        pl.BlockSpec((None, 3, tn), lambda i, j: (layer, 0, j)),
        pl.BlockSpec((None, 1, tn), lambda i, j: (layer, 0, j)),
    ]
    args = [h, h, h, w_up, w_up, conv_w, conv_b.reshape(conv_b.shape[0], 1, f)]
    out_specs = [pl.BlockSpec((tm, tn), lambda i, j: (i, j))]
    out_shape = [jax.ShapeDtypeStruct((m, f), BF16)]
    if side_w is not None:
        side = _SideCast(side_w, layer, (m // tm) * nt, lambda i, j: i * nt + j)
        in_specs.append(side.in_spec)
        args.append(side.arg)
        out_specs.append(side.out_spec)
        out_shape.append(side.out_shape)
    outs = pl.pallas_call(
        _ffn_up_kernel,
        grid=(m // tm, nt),
        in_specs=in_specs,
        out_specs=out_specs,
        out_shape=out_shape,
        scratch_shapes=[pltpu.VMEM((tm + 2 * HALO_ROWS, d), BF16)],
        compiler_params=_params("arbitrary", "arbitrary"),
        name="ffn_up",
    )(*args)
    return (outs[0], outs[1]) if side_w is not None else (outs[0], None)


def _out_proj_kernel(*refs, row, has_next, n_k):
    if has_next:
        a_ref, w_ref, x_ref, g_ref, sh_ref, sc_ref, o_ref, hn_ref = refs
    else:
        a_ref, w_ref, x_ref, g_ref, o_ref = refs
    k = pl.program_id(1)
    last = n_k - 1
    tm = a_ref.shape[0]
    rc = min(OUT_ROW_CHUNK, tm)
    chunks = [slice(c0, c0 + rc) for c0 in range(0, tm, rc)]

    def finish(rows, acc):
        xn = x_ref[rows, :] + g_ref[row:row + 1, :] * acc
        o_ref[rows, :] = xn
        if has_next:
            hn_ref[rows, :] = _modulated(xn, sh_ref, sc_ref, row)

    if n_k == 1:
        for rows in chunks:
            finish(rows, _dot(a_ref[rows, :], w_ref[...]))
        return

    @pl.when(k == 0)
    def _():
        for rows in chunks:
            o_ref[rows, :] = _dot(a_ref[rows, :], w_ref[...])

    @pl.when((k > 0) & (k < last))
    def _():
        for rows in chunks:
            o_ref[rows, :] += _dot(a_ref[rows, :], w_ref[...])

    @pl.when(k == last)
    def _():
        part_next = _dot(a_ref[chunks[0], :], w_ref[...])
        for ci, rows in enumerate(chunks):
            part, part_next = part_next, (_dot(a_ref[chunks[ci + 1], :], w_ref[...])
                                          if ci + 1 < len(chunks) else None)
            finish(rows, o_ref[rows, :] + part)


def _out_proj(a, w_bf16, jl, x, mod, layer, gate_chunk, nxt, row, tm, tk=OUT_K_TILE):
    m, d = x.shape
    kdim = a.shape[1]
    in_specs = [
        pl.BlockSpec((tm, tk), lambda i, k: (i, k)),
        pl.BlockSpec((None, tk, d), lambda i, k: (jl, k, 0)),
        pl.BlockSpec((tm, d), lambda i, k: (i, 0)),
        _mod_spec(layer, gate_chunk, d),
    ]
    args = [a, w_bf16, x, mod]
    out_specs = [pl.BlockSpec((tm, d), lambda i, k: (i, 0))]
    out_shape = [jax.ShapeDtypeStruct((m, d), F32)]
    if nxt is not None:
        in_specs += [_mod_spec(nxt[0], nxt[1], d), _mod_spec(nxt[0], nxt[2], d)]
        args += [mod, mod]
        out_specs.append(pl.BlockSpec((tm, d), lambda i, k: (i, 0)))
        out_shape.append(jax.ShapeDtypeStruct((m, d), BF16))
    outs = pl.pallas_call(
        functools.partial(_out_proj_kernel, row=row, has_next=nxt is not None, n_k=kdim // tk),
        grid=(m // tm, kdim // tk),
        in_specs=in_specs,
        out_specs=out_specs,
        out_shape=out_shape,
        compiler_params=_params("arbitrary", "arbitrary"),
        name="out_proj",
    )(*args)
    return (outs[0], outs[1]) if nxt is not None else (outs[0], None)


def _out_proj_resident_kernel(*refs, row, has_next):
    if has_next:
        a_ref, w_ref, x_ref, g_ref, sh_ref, sc_ref, o_ref, hn_ref = refs
    else:
        a_ref, w_ref, x_ref, g_ref, o_ref = refs
    tm = a_ref.shape[0]
    rc = min(OUT_ROW_CHUNK, tm)
    chunks = [slice(c0, c0 + rc) for c0 in range(0, tm, rc)]
    acc_next = _dot(a_ref[chunks[0], :], w_ref[...])
    for ci, rows in enumerate(chunks):
        acc, acc_next = acc_next, (_dot(a_ref[chunks[ci + 1], :], w_ref[...]) if ci + 1 < len(chunks) else None)
        xn = x_ref[rows, :] + g_ref[row:row + 1, :] * acc
        o_ref[rows, :] = xn
        if has_next:
            hn_ref[rows, :] = _modulated(xn, sh_ref, sc_ref, row)


def _out_proj_resident(a, w_bf16, jl, x, mod, layer, gate_chunk, nxt, row, tm):
    m, d = x.shape
    kdim = a.shape[1]
    in_specs = [
        pl.BlockSpec((tm, kdim), lambda i: (i, 0)),
        pl.BlockSpec((None, kdim, d), lambda i: (jl, 0, 0), pipeline_mode=pl.Buffered(1)),
        pl.BlockSpec((tm, d), lambda i: (i, 0)),
        _mod_spec(layer, gate_chunk, d),
    ]
    args = [a, w_bf16, x, mod]
    out_specs = [pl.BlockSpec((tm, d), lambda i: (i, 0))]
    out_shape = [jax.ShapeDtypeStruct((m, d), F32)]
    if nxt is not None:
        in_specs += [_mod_spec(nxt[0], nxt[1], d), _mod_spec(nxt[0], nxt[2], d)]
        args += [mod, mod]
        out_specs.append(pl.BlockSpec((tm, d), lambda i: (i, 0)))
        out_shape.append(jax.ShapeDtypeStruct((m, d), BF16))
    outs = pl.pallas_call(
        functools.partial(_out_proj_resident_kernel, row=row, has_next=nxt is not None),
        grid=(m // tm,),
        in_specs=in_specs,
        out_specs=out_specs,
        out_shape=out_shape,
        compiler_params=_params("arbitrary"),
        name="out_proj_resident",
    )(*args)
    return (outs[0], outs[1]) if nxt is not None else (outs[0], None)


def _rope_tables(n):
    rows = n // GRID_W
    inv = ROPE_BASE ** (-jnp.arange(ROPE_PAIRS, dtype=F32) / ROPE_PAIRS)
    row_ang = jnp.arange(rows, dtype=F32)[:, None] * inv
    col_ang = jnp.arange(GRID_W, dtype=F32)[:, None] * inv

    def per_token(row_tab, col_tab, lo_sign):
        r = jnp.repeat(row_tab, GRID_W, axis=0)
        c = jnp.tile(col_tab, (rows, 1))
        return jnp.concatenate([lo_sign * r, lo_sign * c, r, c], axis=1)

    return per_token(jnp.cos(row_ang), jnp.cos(col_ang), 1.0), per_token(jnp.sin(row_ang), jnp.sin(col_ang), -1.0)


def kernel(x, c, ctx, c_ctx, w_ada, b_ada, attn_w_qkv, attn_w_o, attn_q_gain, attn_k_gain, attn_sink,
           sc_w_in, sc_conv, sc_w_out, ffn_w_up, ffn_conv, ffn_conv_b, ffn_w_down):
    batch, n, d = x.shape
    l_ctx = ctx.shape[1]
    depth = w_ada.shape[0]
    assert batch == 1 and d == N_KV_HEADS * GROUP * HEAD_DIM
    tm_in = min(ROW_TILE_STREAMED, n)
    tm_out = min(ROW_TILE_OUT, n)
    tm_res = min(ROW_TILE_RESIDENT, n)
    tq = min(QUERY_TILE, n)
    tm_c = l_ctx

    cc = jnp.concatenate([c, c_ctx[None, :], jnp.zeros((MOD_ROWS - 2, d), F32)], axis=0)
    mod = _ada_table(cc, w_ada, b_ada)
    rope_tabs = _rope_tables(n)
    n_qk_heads = N_KV_HEADS * GROUP + N_KV_HEADS
    w_qkv = _paired_qkv_weight(attn_w_qkv, n_qk_heads * HEAD_DIM)
    xs, cs = x[0], ctx[0]
    hx = hc = None

    for l in range(depth):
        is_attn = (l % N_MIXERS) == 0
        j = l // N_MIXERS
        need_ctx = l < depth - 1
        nxt_ffn = (l, SHIFT_FFN, SCALE_FFN)
        next_is_conv = l + 1 < depth and ((l + 1) % N_MIXERS) != 0
        nxt_mix = (l + 1, SHIFT_MIX, SCALE_MIX) if next_is_conv else None
        if is_attn:
            gains = _paired_lanes(jnp.stack([attn_q_gain[j], attn_k_gain[j]]), 1)
            qkv, w_o = _qkv_proj(xs, mod, l, ROW_LATENT, w_qkv, j, gains, rope_tabs, tm_res, side_w=attn_w_o)
            qkv_c, _ = _qkv_proj(cs, mod, l, ROW_CTX, w_qkv, j, gains, None, tm_c)
            o = _band_attention(qkv, qkv_c, attn_sink[j], tq)
            xs, hx = _out_proj_resident(o, w_o[None], 0, xs, mod, l, GATE_MIX, nxt_ffn, ROW_LATENT, tm_res)
            if need_ctx:
                o_c = _ctx_attention(qkv_c, attn_sink[j])
                cs, hc = _out_proj_resident(o_c, w_o[None], 0, cs, mod, l, GATE_MIX, nxt_ffn, ROW_CTX, tm_c)
        else:
            u, w_out = _conv_in(hx, sc_w_in, j, sc_conv, tm_in, side_w=sc_w_out)
            xs, hx = _out_proj_resident(u, w_out[None], 0, xs, mod, l, GATE_MIX, nxt_ffn, ROW_LATENT, tm_res)
            if need_ctx:
                u_c, _ = _conv_in(hc, sc_w_in, j, sc_conv, tm_c, tn=CTX_TILE)
                cs, hc = _out_proj_resident(u_c, w_out[None], 0, cs, mod, l, GATE_MIX, nxt_ffn, ROW_CTX, tm_c)
        a, w_down = _ffn_up(hx, ffn_w_up, l, ffn_conv, ffn_conv_b, tm_in, side_w=ffn_w_down)
        xs, hx = _out_proj(a, w_down[None], 0, xs, mod, l, GATE_FFN, nxt_mix, ROW_LATENT, tm_out)
        if need_ctx:
            a_c, _ = _ffn_up(hc, ffn_w_up, l, ffn_conv, ffn_conv_b, tm_c, tn=CTX_TILE)
            f_half = a_c.shape[1] // 2
            tk_c = f_half if f_half % HEAD_DIM == 0 else CTX_TILE
            cs, hc = _out_proj(a_c, w_down[None], 0, cs, mod, l, GATE_FFN, nxt_mix, ROW_CTX, tm_c, tk=tk_c)
    return xs[None]
```

```python
import functools

import jax
import jax.numpy as jnp
from jax import lax
from jax.experimental import pallas as pl
from jax.experimental.pallas import tpu as pltpu

HEAD_DIM = 128
N_KV_HEADS = 4
GROUP = 4
WINDOW = 128
GRID_W = 64
ROPE_PAIRS = HEAD_DIM // 4
ROPE_BASE = 10000.0
N_MIXERS = 2
EPS = 1e-6
NEG_INF = -1e30

BF16 = jnp.bfloat16
F32 = jnp.float32

V7X_VMEM_LIMIT_BYTES = 56 * 1024 * 1024
HALO_ROWS = 16
F32_ROWS = 8
MOD_ROWS = 8
ROW_LATENT = 0
ROW_CTX = 1
ROW_CHUNK = 512
CTX_TILE = 512
OUT_ROW_CHUNK = 256
ROW_TILE_STREAMED = 2048
ROW_TILE_OUT = 1024
ROW_TILE_RESIDENT = 512
QUERY_TILE = 2048
PROJ_COL_TILE = 256
FFN_COL_TILE = 512
QKV_COL_TILE = 512
OUT_K_TILE = 512
ADA_COL_TILE = 1024
LOG2E = 1.4426950408889634
SCORE_SCALE2 = HEAD_DIM ** -0.5 * LOG2E
SHIFT_MIX, SCALE_MIX, GATE_MIX, SHIFT_FFN, SCALE_FFN, GATE_FFN = range(6)


def _params(*sem):
    return pltpu.CompilerParams(dimension_semantics=sem, vmem_limit_bytes=V7X_VMEM_LIMIT_BYTES)


def _dot(a, b):
    return jnp.dot(a, b, preferred_element_type=F32)


def _ada_kernel(cc_ref, w_ref, b_ref, o_ref):
    a = jax.nn.silu(cc_ref[...]).astype(BF16)
    o_ref[...] = _dot(a, w_ref[...].astype(BF16)) + b_ref[...]


def _ada_table(cc, w_ada, b_ada, tn=ADA_COL_TILE):
    depth, d, d6 = w_ada.shape
    return pl.pallas_call(
        _ada_kernel,
        grid=(depth, d6 // tn),
        in_specs=[
            pl.BlockSpec((MOD_ROWS, d), lambda l, j: (0, 0)),
            pl.BlockSpec((None, d, tn), lambda l, j: (l, 0, j)),
            pl.BlockSpec((None, 1, tn), lambda l, j: (l, 0, j)),
        ],
        out_specs=pl.BlockSpec((None, MOD_ROWS, tn), lambda l, j: (l, 0, j)),
        out_shape=jax.ShapeDtypeStruct((depth, MOD_ROWS, d6), F32),
        compiler_params=_params("arbitrary", "arbitrary"),
        name="ada_table",
    )(cc, w_ada, b_ada.reshape(depth, 1, d6))


def _modulated(x, sh_ref, sc_ref, row):
    r = lax.rsqrt(jnp.mean(x * x, axis=-1, keepdims=True) + EPS)
    return ((x * r) * (1 + sc_ref[row:row + 1, :]) + sh_ref[row:row + 1, :]).astype(BF16)


def _mod_spec(layer, chunk, d):
    return pl.BlockSpec((None, MOD_ROWS, d), lambda *_: (layer, 0, chunk))


def _fill_halo_lhs(h_ref, hp_ref, hn_ref, hs_ref):
    i = pl.program_id(0)
    tm = h_ref.shape[0]
    zero = jnp.zeros(hp_ref.shape, BF16)
    hs_ref[0:HALO_ROWS, :] = jnp.where(i > 0, hp_ref[...], zero)
    hs_ref[HALO_ROWS:HALO_ROWS + tm, :] = h_ref[...]
    hs_ref[HALO_ROWS + tm:, :] = jnp.where(i < pl.num_programs(0) - 1, hn_ref[...], zero)


def _halo_lhs_specs(tm, d, m):
    per = tm // HALO_ROWS
    nblk = m // HALO_ROWS
    return [
        pl.BlockSpec((tm, d), lambda i, j: (i, 0)),
        pl.BlockSpec((HALO_ROWS, d), lambda i, j: (jnp.maximum(i * per - 1, 0), 0)),
        pl.BlockSpec((HALO_ROWS, d), lambda i, j: (jnp.minimum((i + 1) * per, nblk - 1), 0)),
    ]


class _SideCast:
    def __init__(self, w, layer, n_steps, step_of):
        _, kdim, d = w.shape
        rows = kdim // n_steps
        assert kdim % n_steps == 0 and rows % HALO_ROWS == 0
        self.arg = w
        self.in_spec = pl.BlockSpec((None, rows, d), lambda *g: (layer, step_of(*g), 0))
        self.out_spec = pl.BlockSpec((rows, d), lambda *g: (step_of(*g), 0))
        self.out_shape = jax.ShapeDtypeStruct((kdim, d), BF16)


def _row_chunks(tm):
    rc = min(ROW_CHUNK, tm)
    return [(c0, rc) for c0 in range(0, tm, rc)]


def _chunked_conv3(hs_ref, tm, project, w_ref, finish):
    chunks = _row_chunks(tm)
    last = len(chunks) - 1
    ys = []

    def body(ci):
        off = HALO_ROWS if ci == 0 else 0
        return ys[ci][off:off + chunks[ci][1]]

    def emit(ci):
        c0, rc = chunks[ci]
        before = ys[0][HALO_ROWS - F32_ROWS:HALO_ROWS] if ci == 0 else body(ci - 1)[-F32_ROWS:]
        if ci == last:
            off = HALO_ROWS if ci == 0 else 0
            after = ys[ci][off + rc:off + rc + F32_ROWS]
        else:
            after = body(ci + 1)[:F32_ROWS]
        ext = jnp.concatenate([before, body(ci), after], axis=0)
        sl = slice(F32_ROWS, F32_ROWS + rc)
        dn = pltpu.roll(ext, 1, 0)[sl]
        up = pltpu.roll(ext, ext.shape[0] - 1, 0)[sl]
        finish(c0, rc, dn * w_ref[0:1, :] + ext[sl] * w_ref[1:2, :] + up * w_ref[2:3, :])

    for ci, (c0, rc) in enumerate(chunks):
        lo = c0 + (0 if ci == 0 else HALO_ROWS)
        hi = c0 + rc + HALO_ROWS + (HALO_ROWS if ci == last else 0)
        ys.append(project(hs_ref[lo:hi, :]))
        if ci >= 1:
            emit(ci - 1)
    emit(last)


def _paired_lanes(a, n_heads):
    lead = a.shape[:-1]
    return a.reshape(*lead, n_heads, 2, 2, ROPE_PAIRS).swapaxes(-3, -2).reshape(*lead, n_heads * HEAD_DIM)


def _pair_weight_kernel(w_ref, p_ref, o_ref):
    o_ref[...] = _dot(w_ref[...].astype(BF16), p_ref[...]).astype(BF16)


def _paired_qkv_weight(w_qkv, n_qk_cols, tn=QKV_COL_TILE):
    layers, d, n_out = w_qkv.shape
    eye = jnp.eye(tn, dtype=BF16)
    perms = jnp.stack([_paired_lanes(eye, tn // HEAD_DIM), eye])
    n_qk_tiles = n_qk_cols // tn
    return pl.pallas_call(
        _pair_weight_kernel,
        grid=(layers, n_out // tn),
        in_specs=[
            pl.BlockSpec((None, d, tn), lambda l, j: (l, 0, j)),
            pl.BlockSpec((None, tn, tn), lambda l, j: (jnp.where(j < n_qk_tiles, 0, 1), 0, 0)),
        ],
        out_specs=pl.BlockSpec((None, d, tn), lambda l, j: (l, 0, j)),
        out_shape=jax.ShapeDtypeStruct((layers, d, n_out), BF16),
        compiler_params=_params("arbitrary", "arbitrary"),
        name="pair_qkv_weight",
    )(w_qkv, perms)


def _qkv_kernel(*refs, rope, side, row, tn, n_q_tiles, n_qk_tiles):
    refs = list(refs)
    if rope:
        tab_ref = refs.pop()
    if side:
        side_out_ref = refs.pop()
        side_in_ref = refs.pop(-2)
        side_out_ref[...] = side_in_ref[...].astype(BF16)
    x_ref, sh_ref, sc_ref = refs[:3]
    if rope:
        w_ref, gain_ref, cos_ref, sin_ref, o_ref = refs[3:]
    else:
        w_ref, gain_ref, o_ref = refs[3:]
    h = _modulated(x_ref[...], sh_ref, sc_ref, row)
    n_tiles = w_ref.shape[1] // tn
    gain_q = gain_ref[0:1, :] * SCORE_SCALE2
    gain_k = gain_ref[1:2, :]
    if rope:
        for t, gain in enumerate((gain_q, gain_k)):
            partner = pltpu.roll(jnp.broadcast_to(gain, (F32_ROWS, HEAD_DIM)), HEAD_DIM // 2, 1)[0:1, :]
            tab_ref[2 * t] = cos_ref[...] * gain
            tab_ref[2 * t + 1] = sin_ref[...] * partner
    y_next = _dot(h, w_ref[:, 0:tn])
    for jt in range(n_tiles):
        y, y_next = y_next, (_dot(h, w_ref[:, (jt + 1) * tn:(jt + 2) * tn]) if jt + 1 < n_tiles else None)
        if jt >= n_qk_tiles:
            o_ref[:, jt * tn:(jt + 1) * tn] = y.astype(BF16)
            continue
        is_q = jt < n_q_tiles
        for hh in range(tn // HEAD_DIM):
            yh = y[:, hh * HEAD_DIM:(hh + 1) * HEAD_DIM]
            r = lax.rsqrt(jnp.mean(yh * yh, axis=-1, keepdims=True) + EPS)
            if rope:
                t = 0 if is_q else 1
                yh = (yh * tab_ref[2 * t] + pltpu.roll(yh, HEAD_DIM // 2, 1) * tab_ref[2 * t + 1]) * r
            else:
                yh = (yh * r) * (gain_q if is_q else gain_k)
            o_ref[:, jt * tn + hh * HEAD_DIM:jt * tn + (hh + 1) * HEAD_DIM] = yh.astype(BF16)


def _qkv_proj(x, mod, layer, row, w_qkv_paired, jl, gains_paired, rope_tabs, tm, side_w=None, tn=QKV_COL_TILE):
    m, d = x.shape
    n_out = w_qkv_paired.shape[2]
    d_q = d
    d_kv = (n_out - d_q) // 2
    rope = rope_tabs is not None
    in_specs = [
        pl.BlockSpec((tm, d), lambda i: (i, 0)),
        _mod_spec(layer, SHIFT_MIX, d),
        _mod_spec(layer, SCALE_MIX, d),
        pl.BlockSpec((None, d, n_out), lambda i: (jl, 0, 0), pipeline_mode=pl.Buffered(1)),
        pl.BlockSpec((2, HEAD_DIM), lambda i: (0, 0)),
    ]
    args = [x, mod, mod, w_qkv_paired, gains_paired]
    if rope:
        in_specs += [pl.BlockSpec((tm, HEAD_DIM), lambda i: (i, 0))] * 2
        args += list(rope_tabs)
    out_specs = [pl.BlockSpec((tm, n_out), lambda i: (i, 0))]
    out_shape = [jax.ShapeDtypeStruct((m, n_out), BF16)]
    if side_w is not None:
        side = _SideCast(side_w, jl, m // tm, lambda i: i)
        in_specs.append(side.in_spec)
        args.append(side.arg)
        out_specs.append(side.out_spec)
        out_shape.append(side.out_shape)
    kern = functools.partial(_qkv_kernel, rope=rope, side=side_w is not None, row=row, tn=tn,
                             n_q_tiles=d_q // tn, n_qk_tiles=(d_q + d_kv) // tn)
    outs = pl.pallas_call(
        kern,
        grid=(m // tm,),
        in_specs=in_specs,
        out_specs=out_specs,
        out_shape=out_shape,
        scratch_shapes=[pltpu.VMEM((4, tm, HEAD_DIM), F32)] if rope else [],
        compiler_params=_params("arbitrary"),
        name="qkv_proj",
    )(*args)
    return (outs[0], outs[1]) if side_w is not None else (outs[0], None)


def _sink_attend(t, v, sink2):
    rows, n_keys = t.shape
    m2 = jnp.maximum(jnp.broadcast_to(jnp.max(t, axis=-1, keepdims=True), (rows, HEAD_DIM)), sink2)
    e = jnp.exp2(t - jnp.tile(m2, (1, n_keys // HEAD_DIM))).astype(BF16)
    ov = _dot(e, jnp.concatenate([v, jnp.ones_like(v)], axis=1))
    return ov[:, :HEAD_DIM] / (ov[:, HEAD_DIM:] + jnp.exp2(sink2 - m2))


def _scores2(q, k):
    return lax.dot_general(q, k, (((1,), (1,)), ((), ())), preferred_element_type=F32)


def _band_attn_kernel(sink_ref, q_ref, kp_ref, km_ref, kn_ref, kc_ref, vp_ref, vm_ref, vn_ref, vc_ref, o_ref,
                      kw_ref, vw_ref):
    h = pl.program_id(0)
    i = pl.program_id(1)
    tq = q_ref.shape[0]
    nb = tq // WINDOW
    last_blk = pl.num_programs(1) * nb - 1
    kw_ref[0:WINDOW, :] = kp_ref[...]
    kw_ref[WINDOW:WINDOW + tq, :] = km_ref[...]
    kw_ref[WINDOW + tq:, :] = kn_ref[...]
    vw_ref[0:WINDOW, :] = vp_ref[...]
    vw_ref[WINDOW:WINDOW + tq, :] = vm_ref[...]
    vw_ref[WINDOW + tq:, :] = vn_ref[...]
    rows = GROUP * WINDOW
    r = lax.broadcasted_iota(jnp.int32, (rows, WINDOW), 0) & (WINDOW - 1)
    c_minus_r = lax.broadcasted_iota(jnp.int32, (rows, WINDOW), 1) - r
    sink2 = jnp.concatenate([jnp.full((WINDOW, HEAD_DIM), sink_ref[h * GROUP + g] * LOG2E, F32)
                             for g in range(GROUP)], axis=0)
    kc = kc_ref[...]
    vc = vc_ref[...]

    def masked_scores(b):
        blk = i * nb + b
        q4 = jnp.concatenate([q_ref[b * WINDOW:(b + 1) * WINDOW, g * HEAD_DIM:(g + 1) * HEAD_DIM]
                              for g in range(GROUP)], axis=0)
        s = _scores2(q4, jnp.concatenate([kw_ref[b * WINDOW:(b + 3) * WINDOW, :], kc], axis=0))
        lo = jnp.where(blk > 0, 0, WINDOW)
        hi = jnp.where(blk < last_blk, 0, -WINDOW)
        s_prev = jnp.where(c_minus_r >= lo, s[:, 0:WINDOW], NEG_INF)
        s_next = jnp.where(c_minus_r <= hi, s[:, 2 * WINDOW:3 * WINDOW], NEG_INF)
        return jnp.concatenate([s_prev, s[:, WINDOW:2 * WINDOW], s_next, s[:, 3 * WINDOW:]], axis=1)

    s_next = masked_scores(0)
    for b in range(nb):
        s_cur, s_next = s_next, (masked_scores(b + 1) if b + 1 < nb else None)
        v = jnp.concatenate([vw_ref[b * WINDOW:(b + 3) * WINDOW, :], vc], axis=0)
        o = _sink_attend(s_cur, v, sink2)
        for g in range(GROUP):
            o_ref[b * WINDOW:(b + 1) * WINDOW, g * HEAD_DIM:(g + 1) * HEAD_DIM] = (
                o[g * WINDOW:(g + 1) * WINDOW, :].astype(BF16))


def _band_attention(qkv, qkv_c, sink, tq):
    n = qkv.shape[0]
    l_ctx = qkv_c.shape[0]
    d_q = N_KV_HEADS * GROUP * HEAD_DIM
    kcol = d_q // HEAD_DIM
    vcol = kcol + N_KV_HEADS
    per = tq // WINDOW
    nblk = n // WINDOW

    def prev(col0):
        return pl.BlockSpec((WINDOW, HEAD_DIM), lambda h, i: (jnp.maximum(i * per - 1, 0), col0 + h))

    def main(col0):
        return pl.BlockSpec((tq, HEAD_DIM), lambda h, i: (i, col0 + h))

    def nxt(col0):
        return pl.BlockSpec((WINDOW, HEAD_DIM), lambda h, i: (jnp.minimum((i + 1) * per, nblk - 1), col0 + h))

    def ctx(col0):
        return pl.BlockSpec((l_ctx, HEAD_DIM), lambda h, i: (0, col0 + h))

    return pl.pallas_call(
        _band_attn_kernel,
        grid=(N_KV_HEADS, n // tq),
        in_specs=[
            pl.BlockSpec(memory_space=pltpu.SMEM),
            pl.BlockSpec((tq, GROUP * HEAD_DIM), lambda h, i: (i, h)),
            prev(kcol), main(kcol), nxt(kcol), ctx(kcol),
            prev(vcol), main(vcol), nxt(vcol), ctx(vcol),
        ],
        out_specs=pl.BlockSpec((tq, GROUP * HEAD_DIM), lambda h, i: (i, h)),
        out_shape=jax.ShapeDtypeStruct((n, d_q), BF16),
        scratch_shapes=[pltpu.VMEM((tq + 2 * WINDOW, HEAD_DIM), BF16)] * 2,
        compiler_params=_params("arbitrary", "arbitrary"),
        name="band_attention",
    )(sink, qkv, qkv, qkv, qkv, qkv_c, qkv, qkv, qkv, qkv_c)


def _ctx_attn_kernel(sink_ref, q_ref, k_ref, v_ref, o_ref):
    h = pl.program_id(0)
    for g in range(GROUP):
        sl = slice(g * HEAD_DIM, (g + 1) * HEAD_DIM)
        sink2 = jnp.full((q_ref.shape[0], HEAD_DIM), sink_ref[h * GROUP + g] * LOG2E, F32)
        o_ref[:, sl] = _sink_attend(_scores2(q_ref[:, sl], k_ref[...]), v_ref[...], sink2).astype(BF16)


def _ctx_attention(qkv_c, sink):
    l_ctx = qkv_c.shape[0]
    d_q = N_KV_HEADS * GROUP * HEAD_DIM
    kcol = d_q // HEAD_DIM
    vcol = kcol + N_KV_HEADS
    return pl.pallas_call(
        _ctx_attn_kernel,
        grid=(N_KV_HEADS,),
        in_specs=[
            pl.BlockSpec(memory_space=pltpu.SMEM),
            pl.BlockSpec((l_ctx, GROUP * HEAD_DIM), lambda h: (0, h)),
            pl.BlockSpec((l_ctx, HEAD_DIM), lambda h: (0, kcol + h)),
            pl.BlockSpec((l_ctx, HEAD_DIM), lambda h: (0, vcol + h)),
        ],
        out_specs=pl.BlockSpec((l_ctx, GROUP * HEAD_DIM), lambda h: (0, h)),
        out_shape=jax.ShapeDtypeStruct((l_ctx, d_q), BF16),
        compiler_params=_params("arbitrary"),
        name="ctx_attention",
    )(sink, qkv_c, qkv_c, qkv_c)


def _conv_in_kernel(h_ref, hp_ref, hn_ref, wb_ref, wc_ref, wv_ref, cw_ref, *rest):
    if len(rest) == 4:
        side_in_ref, u_ref, side_out_ref, hs_ref = rest
        side_out_ref[...] = side_in_ref[...].astype(BF16)
    else:
        u_ref, hs_ref = rest

    @pl.when(pl.program_id(1) == 0)
    def _():
        _fill_halo_lhs(h_ref, hp_ref, hn_ref, hs_ref)

    wb = wb_ref[...].astype(BF16)
    wc = wc_ref[...].astype(BF16)
    wv = wv_ref[...].astype(BF16)
    def finish(c0, rc, p_conv):
        gate_b = _dot(hs_ref[c0 + HALO_ROWS:c0 + HALO_ROWS + rc, :], wb)
        u_ref[c0:c0 + rc, :] = (gate_b * p_conv).astype(BF16)

    _chunked_conv3(hs_ref, h_ref.shape[0], lambda lhs: _dot(lhs, wc) * _dot(lhs, wv), cw_ref, finish)


def _conv_in(h, w_in, jl, conv_w, tm, side_w=None, tn=PROJ_COL_TILE):
    m, d = h.shape
    nt = d // tn

    def wspec(part):
        return pl.BlockSpec((None, d, tn), lambda i, j: (jl, 0, part * nt + j))

    in_specs = _halo_lhs_specs(tm, d, m) + [wspec(0), wspec(1), wspec(2),
                                            pl.BlockSpec((None, 3, tn), lambda i, j: (jl, 0, j))]
    args = [h, h, h, w_in, w_in, w_in, conv_w]
    out_specs = [pl.BlockSpec((tm, tn), lambda i, j: (i, j))]
    out_shape = [jax.ShapeDtypeStruct((m, d), BF16)]
    if side_w is not None:
        side = _SideCast(side_w, jl, (m // tm) * nt, lambda i, j: i * nt + j)
        in_specs.append(side.in_spec)
        args.append(side.arg)
        out_specs.append(side.out_spec)
        out_shape.append(side.out_shape)
    outs = pl.pallas_call(
        _conv_in_kernel,
        grid=(m // tm, nt),
        in_specs=in_specs,
        out_specs=out_specs,
        out_shape=out_shape,
        scratch_shapes=[pltpu.VMEM((tm + 2 * HALO_ROWS, d), BF16)],
        compiler_params=_params("arbitrary", "arbitrary"),
        name="conv_in",
    )(*args)
    return (outs[0], outs[1]) if side_w is not None else (outs[0], None)


def _ffn_up_kernel(h_ref, hp_ref, hn_ref, wg_ref, wv_ref, cw_ref, cb_ref, *rest):
    if len(rest) == 4:
        side_in_ref, a_ref, side_out_ref, hs_ref = rest
        side_out_ref[...] = side_in_ref[...].astype(BF16)
    else:
        a_ref, hs_ref = rest

    @pl.when(pl.program_id(1) == 0)
    def _():
        _fill_halo_lhs(h_ref, hp_ref, hn_ref, hs_ref)

    wg = wg_ref[...].astype(BF16)
    wv = wv_ref[...].astype(BF16)

    def finish(c0, rc, gate_conv):
        val = _dot(hs_ref[c0 + HALO_ROWS:c0 + HALO_ROWS + rc, :], wv)
        a_ref[c0:c0 + rc, :] = (jax.nn.silu(gate_conv + cb_ref[...]) * val).astype(BF16)

    _chunked_conv3(hs_ref, h_ref.shape[0], lambda lhs: _dot(lhs, wg), cw_ref, finish)


def _ffn_up(h, w_up, layer, conv_w, conv_b, tm, side_w=None, tn=PROJ_COL_TILE):
    m, d = h.shape
    f = w_up.shape[2] // 2
    nt = f // tn
    in_specs = _halo_lhs_specs(tm, d, m) + [
        pl.BlockSpec((None, d, tn), lambda i, j: (layer, 0, j)),
        pl.BlockSpec((None, d, tn), lambda i, j: (layer, 0, nt + j)),
        pl.BlockSpec((None, 3, tn), lambda i, j: (layer, 0, j)),
        pl.BlockSpec((None, 1, tn), lambda i, j: (layer, 0, j)),
    ]
    args = [h, h, h, w_up, w_up, conv_w, conv_b.reshape(conv_b.shape[0], 1, f)]
    out_specs = [pl.BlockSpec((tm, tn), lambda i, j: (i, j))]
    out_shape = [jax.ShapeDtypeStruct((m, f), BF16)]
    if side_w is not None:
        side = _SideCast(side_w, layer, (m // tm) * nt, lambda i, j: i * nt + j)
        in_specs.append(side.in_spec)
        args.append(side.arg)
        out_specs.append(side.out_spec)
        out_shape.append(side.out_shape)
    outs = pl.pallas_call(
        _ffn_up_kernel,
        grid=(m // tm, nt),
        in_specs=in_specs,
        out_specs=out_specs,
        out_shape=out_shape,
        scratch_shapes=[pltpu.VMEM((tm + 2 * HALO_ROWS, d), BF16)],
        compiler_params=_params("arbitrary", "arbitrary"),
        name="ffn_up",
    )(*args)
    return (outs[0], outs[1]) if side_w is not None else (outs[0], None)


def _out_proj_kernel(*refs, row, has_next, n_k):
    if has_next:
        a_ref, w_ref, x_ref, g_ref, sh_ref, sc_ref, o_ref, hn_ref = refs
    else:
        a_ref, w_ref, x_ref, g_ref, o_ref = refs
    k = pl.program_id(1)
    last = n_k - 1
    tm = a_ref.shape[0]
    rc = min(OUT_ROW_CHUNK, tm)
    chunks = [slice(c0, c0 + rc) for c0 in range(0, tm, rc)]

    def finish(rows, acc):
        xn = x_ref[rows, :] + g_ref[row:row + 1, :] * acc
        o_ref[rows, :] = xn
        if has_next:
            hn_ref[rows, :] = _modulated(xn, sh_ref, sc_ref, row)

    if n_k == 1:
        for rows in chunks:
            finish(rows, _dot(a_ref[rows, :], w_ref[...]))
        return

    @pl.when(k == 0)
    def _():
        for rows in chunks:
            o_ref[rows, :] = _dot(a_ref[rows, :], w_ref[...])

    @pl.when((k > 0) & (k < last))
    def _():
        for rows in chunks:
            o_ref[rows, :] += _dot(a_ref[rows, :], w_ref[...])

    @pl.when(k == last)
    def _():
        part_next = _dot(a_ref[chunks[0], :], w_ref[...])
        for ci, rows in enumerate(chunks):
            part, part_next = part_next, (_dot(a_ref[chunks[ci + 1], :], w_ref[...])
                                          if ci + 1 < len(chunks) else None)
            finish(rows, o_ref[rows, :] + part)


def _out_proj(a, w_bf16, jl, x, mod, layer, gate_chunk, nxt, row, tm, tk=OUT_K_TILE):
    m, d = x.shape
    kdim = a.shape[1]
    in_specs = [
        pl.BlockSpec((tm, tk), lambda i, k: (i, k)),
        pl.BlockSpec((None, tk, d), lambda i, k: (jl, k, 0)),
        pl.BlockSpec((tm, d), lambda i, k: (i, 0)),
        _mod_spec(layer, gate_chunk, d),
    ]
    args = [a, w_bf16, x, mod]
    out_specs = [pl.BlockSpec((tm, d), lambda i, k: (i, 0))]
    out_shape = [jax.ShapeDtypeStruct((m, d), F32)]
    if nxt is not None:
        in_specs += [_mod_spec(nxt[0], nxt[1], d), _mod_spec(nxt[0], nxt[2], d)]
        args += [mod, mod]
        out_specs.append(pl.BlockSpec((tm, d), lambda i, k: (i, 0)))
        out_shape.append(jax.ShapeDtypeStruct((m, d), BF16))
    outs = pl.pallas_call(
        functools.partial(_out_proj_kernel, row=row, has_next=nxt is not None, n_k=kdim // tk),
        grid=(m // tm, kdim // tk),
        in_specs=in_specs,
        out_specs=out_specs,
        out_shape=out_shape,
        compiler_params=_params("arbitrary", "arbitrary"),
        name="out_proj",
    )(*args)
    return (outs[0], outs[1]) if nxt is not None else (outs[0], None)


def _out_proj_resident_kernel(*refs, row, has_next):
    if has_next:
        a_ref, w_ref, x_ref, g_ref, sh_ref, sc_ref, o_ref, hn_ref = refs
    else:
        a_ref, w_ref, x_ref, g_ref, o_ref = refs
    tm = a_ref.shape[0]
    rc = min(OUT_ROW_CHUNK, tm)
    chunks = [slice(c0, c0 + rc) for c0 in range(0, tm, rc)]
    acc_next = _dot(a_ref[chunks[0], :], w_ref[...])
    for ci, rows in enumerate(chunks):
        acc, acc_next = acc_next, (_dot(a_ref[chunks[ci + 1], :], w_ref[...]) if ci + 1 < len(chunks) else None)
        xn = x_ref[rows, :] + g_ref[row:row + 1, :] * acc
        o_ref[rows, :] = xn
        if has_next:
            hn_ref[rows, :] = _modulated(xn, sh_ref, sc_ref, row)


def _out_proj_resident(a, w_bf16, jl, x, mod, layer, gate_chunk, nxt, row, tm):
    m, d = x.shape
    kdim = a.shape[1]
    in_specs = [
        pl.BlockSpec((tm, kdim), lambda i: (i, 0)),
        pl.BlockSpec((None, kdim, d), lambda i: (jl, 0, 0), pipeline_mode=pl.Buffered(1)),
        pl.BlockSpec((tm, d), lambda i: (i, 0)),
        _mod_spec(layer, gate_chunk, d),
    ]
    args = [a, w_bf16, x, mod]
    out_specs = [pl.BlockSpec((tm, d), lambda i: (i, 0))]
    out_shape = [jax.ShapeDtypeStruct((m, d), F32)]
    if nxt is not None:
        in_specs += [_mod_spec(nxt[0], nxt[1], d), _mod_spec(nxt[0], nxt[2], d)]
        args += [mod, mod]
        out_specs.append(pl.BlockSpec((tm, d), lambda i: (i, 0)))
        out_shape.append(jax.ShapeDtypeStruct((m, d), BF16))
    outs = pl.pallas_call(
        functools.partial(_out_proj_resident_kernel, row=row, has_next=nxt is not None),
        grid=(m // tm,),
        in_specs=in_specs,
        out_specs=out_specs,
        out_shape=out_shape,
        compiler_params=_params("arbitrary"),
        name="out_proj_resident",
    )(*args)
    return (outs[0], outs[1]) if nxt is not None else (outs[0], None)


def _rope_tables(n):
    rows = n // GRID_W
    inv = ROPE_BASE ** (-jnp.arange(ROPE_PAIRS, dtype=F32) / ROPE_PAIRS)
    row_ang = jnp.arange(rows, dtype=F32)[:, None] * inv
    col_ang = jnp.arange(GRID_W, dtype=F32)[:, None] * inv

    def per_token(row_tab, col_tab, lo_sign):
        r = jnp.repeat(row_tab, GRID_W, axis=0)
        c = jnp.tile(col_tab, (rows, 1))
        return jnp.concatenate([lo_sign * r, lo_sign * c, r, c], axis=1)

    return per_token(jnp.cos(row_ang), jnp.cos(col_ang), 1.0), per_token(jnp.sin(row_ang), jnp.sin(col_ang), -1.0)


def kernel(x, c, ctx, c_ctx, w_ada, b_ada, attn_w_qkv, attn_w_o, attn_q_gain, attn_k_gain, attn_sink,
           sc_w_in, sc_conv, sc_w_out, ffn_w_up, ffn_conv, ffn_conv_b, ffn_w_down):
    batch, n, d = x.shape
    l_ctx = ctx.shape[1]
    depth = w_ada.shape[0]
    assert batch == 1 and d == N_KV_HEADS * GROUP * HEAD_DIM
    tm_in = min(ROW_TILE_STREAMED, n)
    tm_out = min(ROW_TILE_OUT, n)
    tm_res = min(ROW_TILE_RESIDENT, n)
    tq = min(QUERY_TILE, n)
    tm_c = l_ctx

    cc = jnp.concatenate([c, c_ctx[None, :], jnp.zeros((MOD_ROWS - 2, d), F32)], axis=0)
    mod = _ada_table(cc, w_ada, b_ada)
    rope_tabs = _rope_tables(n)
    n_qk_heads = N_KV_HEADS * GROUP + N_KV_HEADS
    w_qkv = _paired_qkv_weight(attn_w_qkv, n_qk_heads * HEAD_DIM)
    xs, cs = x[0], ctx[0]
    hx = hc = None

    for l in range(depth):
        is_attn = (l % N_MIXERS) == 0
        j = l // N_MIXERS
        need_ctx = l < depth - 1
        nxt_ffn = (l, SHIFT_FFN, SCALE_FFN)
        next_is_conv = l + 1 < depth and ((l + 1) % N_MIXERS) != 0
        nxt_mix = (l + 1, SHIFT_MIX, SCALE_MIX) if next_is_conv else None
        if is_attn:
            gains = _paired_lanes(jnp.stack([attn_q_gain[j], attn_k_gain[j]]), 1)
            qkv, w_o = _qkv_proj(xs, mod, l, ROW_LATENT, w_qkv, j, gains, rope_tabs, tm_res, side_w=attn_w_o)
            qkv_c, _ = _qkv_proj(cs, mod, l, ROW_CTX, w_qkv, j, gains, None, tm_c)
            o = _band_attention(qkv, qkv_c, attn_sink[j], tq)
            xs, hx = _out_proj_resident(o, w_o[None], 0, xs, mod, l, GATE_MIX, nxt_ffn, ROW_LATENT, tm_res)
            if need_ctx:
                o_c = _ctx_attention(qkv_c, attn_sink[j])
                cs, hc = _out_proj_resident(o_c, w_o[None], 0, cs, mod, l, GATE_MIX, nxt_ffn, ROW_CTX, tm_c)
        else:
            u, w_out = _conv_in(hx, sc_w_in, j, sc_conv, tm_in, side_w=sc_w_out)
            xs, hx = _out_proj_resident(u, w_out[None], 0, xs, mod, l, GATE_MIX, nxt_ffn, ROW_LATENT, tm_res)
            if need_ctx:
                u_c, _ = _conv_in(hc, sc_w_in, j, sc_conv, tm_c, tn=CTX_TILE)
                cs, hc = _out_proj_resident(u_c, w_out[None], 0, cs, mod, l, GATE_MIX, nxt_ffn, ROW_CTX, tm_c)
        a, w_down = _ffn_up(hx, ffn_w_up, l, ffn_conv, ffn_conv_b, tm_in, side_w=ffn_w_down, tn=FFN_COL_TILE)
        xs, hx = _out_proj(a, w_down[None], 0, xs, mod, l, GATE_FFN, nxt_mix, ROW_LATENT, tm_out)
        if need_ctx:
            a_c, _ = _ffn_up(hc, ffn_w_up, l, ffn_conv, ffn_conv_b, tm_c, tn=CTX_TILE)
            f_half = a_c.shape[1] // 2
            tk_c = f_half if f_half % HEAD_DIM == 0 else CTX_TILE
            cs, hc = _out_proj(a_c, w_down[None], 0, cs, mod, l, GATE_FFN, nxt_mix, ROW_CTX, tm_c, tk=tk_c)
    return xs[None]
```

```python
import functools

import jax
import jax.numpy as jnp
from jax import lax
from jax.experimental import pallas as pl
from jax.experimental.pallas import tpu as pltpu

HEAD_DIM = 128
N_KV_HEADS = 4
GROUP = 4
WINDOW = 128
GRID_W = 64
ROPE_PAIRS = HEAD_DIM // 4
ROPE_BASE = 10000.0
N_MIXERS = 2
EPS = 1e-6
NEG_INF = -1e30

BF16 = jnp.bfloat16
F32 = jnp.float32

V7X_VMEM_LIMIT_BYTES = 56 * 1024 * 1024
HALO_ROWS = 16
F32_ROWS = 8
MOD_ROWS = 8
ROW_LATENT = 0
ROW_CTX = 1
ROW_CHUNK = 512
CTX_TILE = 512
OUT_ROW_CHUNK = 256
ROW_TILE_STREAMED = 2048
ROW_TILE_OUT = 1024
ROW_TILE_RESIDENT = 512
QUERY_TILE = 2048
PROJ_COL_TILE = 256
FFN_COL_TILE = 512
QKV_COL_TILE = 512
OUT_K_TILE = 512
ADA_COL_TILE = 1024
LOG2E = 1.4426950408889634
SCORE_SCALE2 = HEAD_DIM ** -0.5 * LOG2E
SHIFT_MIX, SCALE_MIX, GATE_MIX, SHIFT_FFN, SCALE_FFN, GATE_FFN = range(6)


def _params(*sem):
    return pltpu.CompilerParams(dimension_semantics=sem, vmem_limit_bytes=V7X_VMEM_LIMIT_BYTES)


def _dot(a, b):
    return jnp.dot(a, b, preferred_element_type=F32)


def _ada_kernel(cc_ref, w_ref, b_ref, o_ref):
    a = jax.nn.silu(cc_ref[...]).astype(BF16)
    o_ref[...] = _dot(a, w_ref[...].astype(BF16)) + b_ref[...]


def _ada_table(cc, w_ada, b_ada, tn=ADA_COL_TILE):
    depth, d, d6 = w_ada.shape
    return pl.pallas_call(
        _ada_kernel,
        grid=(depth, d6 // tn),
        in_specs=[
            pl.BlockSpec((MOD_ROWS, d), lambda l, j: (0, 0)),
            pl.BlockSpec((None, d, tn), lambda l, j: (l, 0, j)),
            pl.BlockSpec((None, 1, tn), lambda l, j: (l, 0, j)),
        ],
        out_specs=pl.BlockSpec((None, MOD_ROWS, tn), lambda l, j: (l, 0, j)),
        out_shape=jax.ShapeDtypeStruct((depth, MOD_ROWS, d6), F32),
        compiler_params=_params("arbitrary", "arbitrary"),
        name="ada_table",
    )(cc, w_ada, b_ada.reshape(depth, 1, d6))


def _modulated(x, sh_ref, sc_ref, row):
    r = lax.rsqrt(jnp.mean(x * x, axis=-1, keepdims=True) + EPS)
    return ((x * r) * (1 + sc_ref[row:row + 1, :]) + sh_ref[row:row + 1, :]).astype(BF16)


def _mod_spec(layer, chunk, d):
    return pl.BlockSpec((None, MOD_ROWS, d), lambda *_: (layer, 0, chunk))


def _fill_halo_lhs(h_ref, hp_ref, hn_ref, hs_ref):
    i = pl.program_id(0)
    tm = h_ref.shape[0]
    zero = jnp.zeros(hp_ref.shape, BF16)
    hs_ref[0:HALO_ROWS, :] = jnp.where(i > 0, hp_ref[...], zero)
    hs_ref[HALO_ROWS:HALO_ROWS + tm, :] = h_ref[...]
    hs_ref[HALO_ROWS + tm:, :] = jnp.where(i < pl.num_programs(0) - 1, hn_ref[...], zero)


def _halo_lhs_specs(tm, d, m):
    per = tm // HALO_ROWS
    nblk = m // HALO_ROWS
    return [
        pl.BlockSpec((tm, d), lambda i, j: (i, 0)),
        pl.BlockSpec((HALO_ROWS, d), lambda i, j: (jnp.maximum(i * per - 1, 0), 0)),
        pl.BlockSpec((HALO_ROWS, d), lambda i, j: (jnp.minimum((i + 1) * per, nblk - 1), 0)),
    ]


class _SideCast:
    def __init__(self, w, layer, n_steps, step_of):
        _, kdim, d = w.shape
        rows = kdim // n_steps
        assert kdim % n_steps == 0 and rows % HALO_ROWS == 0
        self.arg = w
        self.in_spec = pl.BlockSpec((None, rows, d), lambda *g: (layer, step_of(*g), 0))
        self.out_spec = pl.BlockSpec((rows, d), lambda *g: (step_of(*g), 0))
        self.out_shape = jax.ShapeDtypeStruct((kdim, d), BF16)


def _row_chunks(tm):
    rc = min(ROW_CHUNK, tm)
    return [(c0, rc) for c0 in range(0, tm, rc)]


def _chunked_conv3(hs_ref, tm, project, w_ref, finish):
    chunks = _row_chunks(tm)
    last = len(chunks) - 1
    ys = []

    def body(ci):
        off = HALO_ROWS if ci == 0 else 0
        return ys[ci][off:off + chunks[ci][1]]

    def emit(ci):
        c0, rc = chunks[ci]
        before = ys[0][HALO_ROWS - F32_ROWS:HALO_ROWS] if ci == 0 else body(ci - 1)[-F32_ROWS:]
        if ci == last:
            off = HALO_ROWS if ci == 0 else 0
            after = ys[ci][off + rc:off + rc + F32_ROWS]
        else:
            after = body(ci + 1)[:F32_ROWS]
        ext = jnp.concatenate([before, body(ci), after], axis=0)
        sl = slice(F32_ROWS, F32_ROWS + rc)
        dn = pltpu.roll(ext, 1, 0)[sl]
        up = pltpu.roll(ext, ext.shape[0] - 1, 0)[sl]
        finish(c0, rc, dn * w_ref[0:1, :] + ext[sl] * w_ref[1:2, :] + up * w_ref[2:3, :])

    for ci, (c0, rc) in enumerate(chunks):
        lo = c0 + (0 if ci == 0 else HALO_ROWS)
        hi = c0 + rc + HALO_ROWS + (HALO_ROWS if ci == last else 0)
        ys.append(project(hs_ref[lo:hi, :]))
        if ci >= 1:
            emit(ci - 1)
    emit(last)


def _paired_lanes(a, n_heads):
    lead = a.shape[:-1]
    return a.reshape(*lead, n_heads, 2, 2, ROPE_PAIRS).swapaxes(-3, -2).reshape(*lead, n_heads * HEAD_DIM)


def _pair_weight_kernel(w_ref, p_ref, o_ref):
    o_ref[...] = _dot(w_ref[...].astype(BF16), p_ref[...]).astype(BF16)


def _paired_qkv_weight(w_qkv, n_qk_cols, tn=QKV_COL_TILE):
    layers, d, n_out = w_qkv.shape
    eye = jnp.eye(tn, dtype=BF16)
    perms = jnp.stack([_paired_lanes(eye, tn // HEAD_DIM), eye])
    n_qk_tiles = n_qk_cols // tn
    return pl.pallas_call(
        _pair_weight_kernel,
        grid=(layers, n_out // tn),
        in_specs=[
            pl.BlockSpec((None, d, tn), lambda l, j: (l, 0, j)),
            pl.BlockSpec((None, tn, tn), lambda l, j: (jnp.where(j < n_qk_tiles, 0, 1), 0, 0)),
        ],
        out_specs=pl.BlockSpec((None, d, tn), lambda l, j: (l, 0, j)),
        out_shape=jax.ShapeDtypeStruct((layers, d, n_out), BF16),
        compiler_params=_params("arbitrary", "arbitrary"),
        name="pair_qkv_weight",
    )(w_qkv, perms)


def _qkv_kernel(*refs, rope, side, row, tn, n_q_tiles, n_qk_tiles):
    refs = list(refs)
    if rope:
        tab_ref = refs.pop()
    if side:
        side_out_ref = refs.pop()
        side_in_ref = refs.pop(-2)
        side_out_ref[...] = side_in_ref[...].astype(BF16)
    x_ref, sh_ref, sc_ref = refs[:3]
    if rope:
        w_ref, gain_ref, cos_ref, sin_ref, o_ref = refs[3:]
    else:
        w_ref, gain_ref, o_ref = refs[3:]
    h = _modulated(x_ref[...], sh_ref, sc_ref, row)
    n_tiles = w_ref.shape[1] // tn
    gain_q = gain_ref[0:1, :] * SCORE_SCALE2
    gain_k = gain_ref[1:2, :]
    if rope:
        for t, gain in enumerate((gain_q, gain_k)):
            partner = pltpu.roll(jnp.broadcast_to(gain, (F32_ROWS, HEAD_DIM)), HEAD_DIM // 2, 1)[0:1, :]
            tab_ref[2 * t] = cos_ref[...] * gain
            tab_ref[2 * t + 1] = sin_ref[...] * partner
    y_next = _dot(h, w_ref[:, 0:tn])
    for jt in range(n_tiles):
        y, y_next = y_next, (_dot(h, w_ref[:, (jt + 1) * tn:(jt + 2) * tn]) if jt + 1 < n_tiles else None)
        if jt >= n_qk_tiles:
            o_ref[:, jt * tn:(jt + 1) * tn] = y.astype(BF16)
            continue
        is_q = jt < n_q_tiles
        for hh in range(tn // HEAD_DIM):
            yh = y[:, hh * HEAD_DIM:(hh + 1) * HEAD_DIM]
            r = lax.rsqrt(jnp.mean(yh * yh, axis=-1, keepdims=True) + EPS)
            if rope:
                t = 0 if is_q else 1
                yh = (yh * tab_ref[2 * t] + pltpu.roll(yh, HEAD_DIM // 2, 1) * tab_ref[2 * t + 1]) * r
            else:
                yh = (yh * r) * (gain_q if is_q else gain_k)
            o_ref[:, jt * tn + hh * HEAD_DIM:jt * tn + (hh + 1) * HEAD_DIM] = yh.astype(BF16)


def _qkv_proj(x, mod, layer, row, w_qkv_paired, jl, gains_paired, rope_tabs, tm, side_w=None, tn=QKV_COL_TILE):
    m, d = x.shape
    n_out = w_qkv_paired.shape[2]
    d_q = d
    d_kv = (n_out - d_q) // 2
    rope = rope_tabs is not None
    in_specs = [
        pl.BlockSpec((tm, d), lambda i: (i, 0)),
        _mod_spec(layer, SHIFT_MIX, d),
        _mod_spec(layer, SCALE_MIX, d),
        pl.BlockSpec((None, d, n_out), lambda i: (jl, 0, 0), pipeline_mode=pl.Buffered(1)),
        pl.BlockSpec((2, HEAD_DIM), lambda i: (0, 0)),
    ]
    args = [x, mod, mod, w_qkv_paired, gains_paired]
    if rope:
        in_specs += [pl.BlockSpec((tm, HEAD_DIM), lambda i: (i, 0))] * 2
        args += list(rope_tabs)
    out_specs = [pl.BlockSpec((tm, n_out), lambda i: (i, 0))]
    out_shape = [jax.ShapeDtypeStruct((m, n_out), BF16)]
    if side_w is not None:
        side = _SideCast(side_w, jl, m // tm, lambda i: i)
        in_specs.append(side.in_spec)
        args.append(side.arg)
        out_specs.append(side.out_spec)
        out_shape.append(side.out_shape)
    kern = functools.partial(_qkv_kernel, rope=rope, side=side_w is not None, row=row, tn=tn,
                             n_q_tiles=d_q // tn, n_qk_tiles=(d_q + d_kv) // tn)
    outs = pl.pallas_call(
        kern,
        grid=(m // tm,),
        in_specs=in_specs,
        out_specs=out_specs,
        out_shape=out_shape,
        scratch_shapes=[pltpu.VMEM((4, tm, HEAD_DIM), F32)] if rope else [],
        compiler_params=_params("arbitrary"),
        name="qkv_proj",
    )(*args)
    return (outs[0], outs[1]) if side_w is not None else (outs[0], None)


def _sink_attend(t, v, sink2):
    rows, n_keys = t.shape
    m2 = jnp.maximum(jnp.broadcast_to(jnp.max(t, axis=-1, keepdims=True), (rows, HEAD_DIM)), sink2)
    e = jnp.exp2(t - jnp.tile(m2, (1, n_keys // HEAD_DIM))).astype(BF16)
    ov = _dot(e, jnp.concatenate([v, jnp.ones_like(v)], axis=1))
    return ov[:, :HEAD_DIM] / (ov[:, HEAD_DIM:] + jnp.exp2(sink2 - m2))


def _scores2(q, k):
    return lax.dot_general(q, k, (((1,), (1,)), ((), ())), preferred_element_type=F32)


def _band_attn_kernel(sink_ref, q_ref, kp_ref, km_ref, kn_ref, kc_ref, vp_ref, vm_ref, vn_ref, vc_ref, o_ref,
                      kw_ref, vw_ref):
    h = pl.program_id(0)
    i = pl.program_id(1)
    tq = q_ref.shape[0]
    nb = tq // WINDOW
    last_blk = pl.num_programs(1) * nb - 1
    kw_ref[0:WINDOW, :] = kp_ref[...]
    kw_ref[WINDOW:WINDOW + tq, :] = km_ref[...]
    kw_ref[WINDOW + tq:, :] = kn_ref[...]
    vw_ref[0:WINDOW, :] = vp_ref[...]
    vw_ref[WINDOW:WINDOW + tq, :] = vm_ref[...]
    vw_ref[WINDOW + tq:, :] = vn_ref[...]
    rows = GROUP * WINDOW
    r = lax.broadcasted_iota(jnp.int32, (rows, WINDOW), 0) & (WINDOW - 1)
    c_minus_r = lax.broadcasted_iota(jnp.int32, (rows, WINDOW), 1) - r
    sink2 = jnp.concatenate([jnp.full((WINDOW, HEAD_DIM), sink_ref[h * GROUP + g] * LOG2E, F32)
                             for g in range(GROUP)], axis=0)
    kc = kc_ref[...]
    vc = vc_ref[...]

    def masked_scores(b):
        blk = i * nb + b
        q4 = jnp.concatenate([q_ref[b * WINDOW:(b + 1) * WINDOW, g * HEAD_DIM:(g + 1) * HEAD_DIM]
                              for g in range(GROUP)], axis=0)
        s = _scores2(q4, jnp.concatenate([kw_ref[b * WINDOW:(b + 3) * WINDOW, :], kc], axis=0))
        lo = jnp.where(blk > 0, 0, WINDOW)
        hi = jnp.where(blk < last_blk, 0, -WINDOW)
        s_prev = jnp.where(c_minus_r >= lo, s[:, 0:WINDOW], NEG_INF)
        s_next = jnp.where(c_minus_r <= hi, s[:, 2 * WINDOW:3 * WINDOW], NEG_INF)
        return jnp.concatenate([s_prev, s[:, WINDOW:2 * WINDOW], s_next, s[:, 3 * WINDOW:]], axis=1)

    s_next = masked_scores(0)
    for b in range(nb):
        s_cur, s_next = s_next, (masked_scores(b + 1) if b + 1 < nb else None)
        v = jnp.concatenate([vw_ref[b * WINDOW:(b + 3) * WINDOW, :], vc], axis=0)
        o = _sink_attend(s_cur, v, sink2)
        for g in range(GROUP):
            o_ref[b * WINDOW:(b + 1) * WINDOW, g * HEAD_DIM:(g + 1) * HEAD_DIM] = (
                o[g * WINDOW:(g + 1) * WINDOW, :].astype(BF16))


def _band_attention(qkv, qkv_c, sink, tq):
    n = qkv.shape[0]
    l_ctx = qkv_c.shape[0]
    d_q = N_KV_HEADS * GROUP * HEAD_DIM
    kcol = d_q // HEAD_DIM
    vcol = kcol + N_KV_HEADS
    per = tq // WINDOW
    nblk = n // WINDOW

    def prev(col0):
        return pl.BlockSpec((WINDOW, HEAD_DIM), lambda h, i: (jnp.maximum(i * per - 1, 0), col0 + h))

    def main(col0):
        return pl.BlockSpec((tq, HEAD_DIM), lambda h, i: (i, col0 + h))

    def nxt(col0):
        return pl.BlockSpec((WINDOW, HEAD_DIM), lambda h, i: (jnp.minimum((i + 1) * per, nblk - 1), col0 + h))

    def ctx(col0):
        return pl.BlockSpec((l_ctx, HEAD_DIM), lambda h, i: (0, col0 + h))

    return pl.pallas_call(
        _band_attn_kernel,
        grid=(N_KV_HEADS, n // tq),
        in_specs=[
            pl.BlockSpec(memory_space=pltpu.SMEM),
            pl.BlockSpec((tq, GROUP * HEAD_DIM), lambda h, i: (i, h)),
            prev(kcol), main(kcol), nxt(kcol), ctx(kcol),
            prev(vcol), main(vcol), nxt(vcol), ctx(vcol),
        ],
        out_specs=pl.BlockSpec((tq, GROUP * HEAD_DIM), lambda h, i: (i, h)),
        out_shape=jax.ShapeDtypeStruct((n, d_q), BF16),
        scratch_shapes=[pltpu.VMEM((tq + 2 * WINDOW, HEAD_DIM), BF16)] * 2,
        compiler_params=_params("arbitrary", "arbitrary"),
        name="band_attention",
    )(sink, qkv, qkv, qkv, qkv, qkv_c, qkv, qkv, qkv, qkv_c)


def _ctx_attn_kernel(sink_ref, q_ref, k_ref, v_ref, o_ref):
    h = pl.program_id(0)
    for g in range(GROUP):
        sl = slice(g * HEAD_DIM, (g + 1) * HEAD_DIM)
        sink2 = jnp.full((q_ref.shape[0], HEAD_DIM), sink_ref[h * GROUP + g] * LOG2E, F32)
        o_ref[:, sl] = _sink_attend(_scores2(q_ref[:, sl], k_ref[...]), v_ref[...], sink2).astype(BF16)


def _ctx_attention(qkv_c, sink):
    l_ctx = qkv_c.shape[0]
    d_q = N_KV_HEADS * GROUP * HEAD_DIM
    kcol = d_q // HEAD_DIM
    vcol = kcol + N_KV_HEADS
    return pl.pallas_call(
        _ctx_attn_kernel,
        grid=(N_KV_HEADS,),
        in_specs=[
            pl.BlockSpec(memory_space=pltpu.SMEM),
            pl.BlockSpec((l_ctx, GROUP * HEAD_DIM), lambda h: (0, h)),
            pl.BlockSpec((l_ctx, HEAD_DIM), lambda h: (0, kcol + h)),
            pl.BlockSpec((l_ctx, HEAD_DIM), lambda h: (0, vcol + h)),
        ],
        out_specs=pl.BlockSpec((l_ctx, GROUP * HEAD_DIM), lambda h: (0, h)),
        out_shape=jax.ShapeDtypeStruct((l_ctx, d_q), BF16),
        compiler_params=_params("arbitrary"),
        name="ctx_attention",
    )(sink, qkv_c, qkv_c, qkv_c)


def _conv_in_kernel(h_ref, hp_ref, hn_ref, wb_ref, wc_ref, wv_ref, cw_ref, *rest):
    if len(rest) == 4:
        side_in_ref, u_ref, side_out_ref, hs_ref = rest
        side_out_ref[...] = side_in_ref[...].astype(BF16)
    else:
        u_ref, hs_ref = rest

    @pl.when(pl.program_id(1) == 0)
    def _():
        _fill_halo_lhs(h_ref, hp_ref, hn_ref, hs_ref)

    wb = wb_ref[...].astype(BF16)
    wc = wc_ref[...].astype(BF16)
    wv = wv_ref[...].astype(BF16)
    def finish(c0, rc, p_conv):
        gate_b = _dot(hs_ref[c0 + HALO_ROWS:c0 + HALO_ROWS + rc, :], wb)
        u_ref[c0:c0 + rc, :] = (gate_b * p_conv).astype(BF16)

    _chunked_conv3(hs_ref, h_ref.shape[0], lambda lhs: _dot(lhs, wc) * _dot(lhs, wv), cw_ref, finish)


def _conv_in(h, w_in, jl, conv_w, tm, side_w=None, tn=PROJ_COL_TILE):
    m, d = h.shape
    nt = d // tn

    def wspec(part):
        return pl.BlockSpec((None, d, tn), lambda i, j: (jl, 0, part * nt + j))

    in_specs = _halo_lhs_specs(tm, d, m) + [wspec(0), wspec(1), wspec(2),
                                            pl.BlockSpec((None, 3, tn), lambda i, j: (jl, 0, j))]
    args = [h, h, h, w_in, w_in, w_in, conv_w]
    out_specs = [pl.BlockSpec((tm, tn), lambda i, j: (i, j))]
    out_shape = [jax.ShapeDtypeStruct((m, d), BF16)]
    if side_w is not None:
        side = _SideCast(side_w, jl, (m // tm) * nt, lambda i, j: i * nt + j)
        in_specs.append(side.in_spec)
        args.append(side.arg)
        out_specs.append(side.out_spec)
        out_shape.append(side.out_shape)
    outs = pl.pallas_call(
        _conv_in_kernel,
        grid=(m // tm, nt),
        in_specs=in_specs,
        out_specs=out_specs,
        out_shape=out_shape,
        scratch_shapes=[pltpu.VMEM((tm + 2 * HALO_ROWS, d), BF16)],
        compiler_params=_params("arbitrary", "arbitrary"),
        name="conv_in",
    )(*args)
    return (outs[0], outs[1]) if side_w is not None else (outs[0], None)


def _ffn_up_kernel(h_ref, hp_ref, hn_ref, wg_ref, wv_ref, cw_ref, cb_ref, *rest):
    if len(rest) == 4:
        side_in_ref, a_ref, side_out_ref, hs_ref = rest
        side_out_ref[...] = side_in_ref[...].astype(BF16)
    else:
        a_ref, hs_ref = rest

    @pl.when(pl.program_id(1) == 0)
    def _():
        _fill_halo_lhs(h_ref, hp_ref, hn_ref, hs_ref)

    wg = wg_ref[...].astype(BF16)
    wv = wv_ref[...].astype(BF16)

    def finish(c0, rc, gate_conv):
        val = _dot(hs_ref[c0 + HALO_ROWS:c0 + HALO_ROWS + rc, :], wv)
        a_ref[c0:c0 + rc, :] = (jax.nn.silu(gate_conv + cb_ref[...]) * val).astype(BF16)

    _chunked_conv3(hs_ref, h_ref.shape[0], lambda lhs: _dot(lhs, wg), cw_ref, finish)


def _ffn_up(h, w_up, layer, conv_w, conv_b, tm, side_w=None, tn=PROJ_COL_TILE):
    m, d = h.shape
    f = w_up.shape[2] // 2
    nt = f // tn
    in_specs = _halo_lhs_specs(tm, d, m) + [
        pl.BlockSpec((None, d, tn), lambda i, j: (layer, 0, j)),
        pl.BlockSpec((None, d, tn), lambda i, j: (layer, 0, nt + j)),
        pl.BlockSpec((None, 3, tn), lambda i, j: (layer, 0, j)),
        pl.BlockSpec((None, 1, tn), lambda i, j: (layer, 0, j)),
    ]
    args = [h, h, h, w_up, w_up, conv_w, conv_b.reshape(conv_b.shape[0], 1, f)]
    out_specs = [pl.BlockSpec((tm, tn), lambda i, j: (i, j))]
    out_shape = [jax.ShapeDtypeStruct((m, f), BF16)]
    if side_w is not None:
        side = _SideCast(side_w, layer, (m // tm) * nt, lambda i, j: i * nt + j)
        in_specs.append(side.in_spec)
        args.append(side.arg)
        out_specs.append(side.out_spec)
        out_shape.append(side.out_shape)
    outs = pl.pallas_call(
        _ffn_up_kernel,
        grid=(m // tm, nt),
        in_specs=in_specs,
        out_specs=out_specs,
        out_shape=out_shape,
        scratch_shapes=[pltpu.VMEM((tm + 2 * HALO_ROWS, d), BF16)],
        compiler_params=_params("arbitrary", "arbitrary"),
        name="ffn_up",
    )(*args)
    return (outs[0], outs[1]) if side_w is not None else (outs[0], None)


def _out_proj_kernel(*refs, row, has_next, n_k):
    if has_next:
        a_ref, w_ref, x_ref, g_ref, sh_ref, sc_ref, o_ref, hn_ref = refs
    else:
        a_ref, w_ref, x_ref, g_ref, o_ref = refs
    k = pl.program_id(1)
    last = n_k - 1
    tm = a_ref.shape[0]
    rc = min(OUT_ROW_CHUNK, tm)
    chunks = [slice(c0, c0 + rc) for c0 in range(0, tm, rc)]

    def finish(rows, acc):
        xn = x_ref[rows, :] + g_ref[row:row + 1, :] * acc
        o_ref[rows, :] = xn
        if has_next:
            hn_ref[rows, :] = _modulated(xn, sh_ref, sc_ref, row)

    if n_k == 1:
        for rows in chunks:
            finish(rows, _dot(a_ref[rows, :], w_ref[...]))
        return

    @pl.when(k == 0)
    def _():
        for rows in chunks:
            o_ref[rows, :] = _dot(a_ref[rows, :], w_ref[...])

    @pl.when((k > 0) & (k < last))
    def _():
        for rows in chunks:
            o_ref[rows, :] += _dot(a_ref[rows, :], w_ref[...])

    @pl.when(k == last)
    def _():
        part_next = _dot(a_ref[chunks[0], :], w_ref[...])
        for ci, rows in enumerate(chunks):
            part, part_next = part_next, (_dot(a_ref[chunks[ci + 1], :], w_ref[...])
                                          if ci + 1 < len(chunks) else None)
            finish(rows, o_ref[rows, :] + part)


def _out_proj(a, w_bf16, jl, x, mod, layer, gate_chunk, nxt, row, tm, tk=OUT_K_TILE):
    m, d = x.shape
    kdim = a.shape[1]
    in_specs = [
        pl.BlockSpec((tm, tk), lambda i, k: (i, k)),
        pl.BlockSpec((None, tk, d), lambda i, k: (jl, k, 0)),
        pl.BlockSpec((tm, d), lambda i, k: (i, 0)),
        _mod_spec(layer, gate_chunk, d),
    ]
    args = [a, w_bf16, x, mod]
    out_specs = [pl.BlockSpec((tm, d), lambda i, k: (i, 0))]
    out_shape = [jax.ShapeDtypeStruct((m, d), F32)]
    if nxt is not None:
        in_specs += [_mod_spec(nxt[0], nxt[1], d), _mod_spec(nxt[0], nxt[2], d)]
        args += [mod, mod]
        out_specs.append(pl.BlockSpec((tm, d), lambda i, k: (i, 0)))
        out_shape.append(jax.ShapeDtypeStruct((m, d), BF16))
    outs = pl.pallas_call(
        functools.partial(_out_proj_kernel, row=row, has_next=nxt is not None, n_k=kdim // tk),
        grid=(m // tm, kdim // tk),
        in_specs=in_specs,
        out_specs=out_specs,
        out_shape=out_shape,
        compiler_params=_params("arbitrary", "arbitrary"),
        name="out_proj",
    )(*args)
    return (outs[0], outs[1]) if nxt is not None else (outs[0], None)


def _out_proj_resident_kernel(*refs, row, has_next):
    if has_next:
        a_ref, w_ref, x_ref, g_ref, sh_ref, sc_ref, o_ref, hn_ref = refs
    else:
        a_ref, w_ref, x_ref, g_ref, o_ref = refs
    tm = a_ref.shape[0]
    rc = min(OUT_ROW_CHUNK, tm)
    chunks = [slice(c0, c0 + rc) for c0 in range(0, tm, rc)]
    acc_next = _dot(a_ref[chunks[0], :], w_ref[...])
    for ci, rows in enumerate(chunks):
        acc, acc_next = acc_next, (_dot(a_ref[chunks[ci + 1], :], w_ref[...]) if ci + 1 < len(chunks) else None)
        xn = x_ref[rows, :] + g_ref[row:row + 1, :] * acc
        o_ref[rows, :] = xn
        if has_next:
            hn_ref[rows, :] = _modulated(xn, sh_ref, sc_ref, row)


def _out_proj_resident(a, w_bf16, jl, x, mod, layer, gate_chunk, nxt, row, tm):
    m, d = x.shape
    kdim = a.shape[1]
    in_specs = [
        pl.BlockSpec((tm, kdim), lambda i: (i, 0)),
        pl.BlockSpec((None, kdim, d), lambda i: (jl, 0, 0), pipeline_mode=pl.Buffered(1)),
        pl.BlockSpec((tm, d), lambda i: (i, 0)),
        _mod_spec(layer, gate_chunk, d),
    ]
    args = [a, w_bf16, x, mod]
    out_specs = [pl.BlockSpec((tm, d), lambda i: (i, 0))]
    out_shape = [jax.ShapeDtypeStruct((m, d), F32)]
    if nxt is not None:
        in_specs += [_mod_spec(nxt[0], nxt[1], d), _mod_spec(nxt[0], nxt[2], d)]
        args += [mod, mod]
        out_specs.append(pl.BlockSpec((tm, d), lambda i: (i, 0)))
        out_shape.append(jax.ShapeDtypeStruct((m, d), BF16))
    outs = pl.pallas_call(
        functools.partial(_out_proj_resident_kernel, row=row, has_next=nxt is not None),
        grid=(m // tm,),
        in_specs=in_specs,
        out_specs=out_specs,
        out_shape=out_shape,
        compiler_params=_params("arbitrary"),
        name="out_proj_resident",
    )(*args)
    return (outs[0], outs[1]) if nxt is not None else (outs[0], None)


def _rope_tables(n):
    rows = n // GRID_W
    inv = ROPE_BASE ** (-jnp.arange(ROPE_PAIRS, dtype=F32) / ROPE_PAIRS)
    row_ang = jnp.arange(rows, dtype=F32)[:, None] * inv
    col_ang = jnp.arange(GRID_W, dtype=F32)[:, None] * inv

    def per_token(row_tab, col_tab, lo_sign):
        r = jnp.repeat(row_tab, GRID_W, axis=0)
        c = jnp.tile(col_tab, (rows, 1))
        return jnp.concatenate([lo_sign * r, lo_sign * c, r, c], axis=1)

    return per_token(jnp.cos(row_ang), jnp.cos(col_ang), 1.0), per_token(jnp.sin(row_ang), jnp.sin(col_ang), -1.0)


def kernel(x, c, ctx, c_ctx, w_ada, b_ada, attn_w_qkv, attn_w_o, attn_q_gain, attn_k_gain, attn_sink,
           sc_w_in, sc_conv, sc_w_out, ffn_w_up, ffn_conv, ffn_conv_b, ffn_w_down):
    batch, n, d = x.shape
    l_ctx = ctx.shape[1]
    depth = w_ada.shape[0]
    assert batch == 1 and d == N_KV_HEADS * GROUP * HEAD_DIM
    tm_in = min(ROW_TILE_STREAMED, n)
    tm_out = min(ROW_TILE_OUT, n)
    tm_res = min(ROW_TILE_RESIDENT, n)
    tq = min(QUERY_TILE, n)
    tm_c = l_ctx

    cc = jnp.concatenate([c, c_ctx[None, :], jnp.zeros((MOD_ROWS - 2, d), F32)], axis=0)
    mod = _ada_table(cc, w_ada, b_ada)
    rope_tabs = _rope_tables(n)
    n_qk_heads = N_KV_HEADS * GROUP + N_KV_HEADS
    w_qkv = _paired_qkv_weight(attn_w_qkv, n_qk_heads * HEAD_DIM)
    xs, cs = x[0], ctx[0]
    hx = hc = None

    for l in range(depth):
        is_attn = (l % N_MIXERS) == 0
        j = l // N_MIXERS
        need_ctx = l < depth - 1
        nxt_ffn = (l, SHIFT_FFN, SCALE_FFN)
        next_is_conv = l + 1 < depth and ((l + 1) % N_MIXERS) != 0
        nxt_mix = (l + 1, SHIFT_MIX, SCALE_MIX) if next_is_conv else None
        if is_attn:
            gains = _paired_lanes(jnp.stack([attn_q_gain[j], attn_k_gain[j]]), 1)
            qkv, w_o = _qkv_proj(xs, mod, l, ROW_LATENT, w_qkv, j, gains, rope_tabs, tm_res, side_w=attn_w_o)
            qkv_c, _ = _qkv_proj(cs, mod, l, ROW_CTX, w_qkv, j, gains, None, tm_c)
            o = _band_attention(qkv, qkv_c, attn_sink[j], tq)
            xs, hx = _out_proj_resident(o, w_o[None], 0, xs, mod, l, GATE_MIX, nxt_ffn, ROW_LATENT, tm_res)
            if need_ctx:
                o_c = _ctx_attention(qkv_c, attn_sink[j])
                cs, hc = _out_proj_resident(o_c, w_o[None], 0, cs, mod, l, GATE_MIX, nxt_ffn, ROW_CTX, tm_c)
        else:
            u, w_out = _conv_in(hx, sc_w_in, j, sc_conv, tm_in, side_w=sc_w_out)
            xs, hx = _out_proj_resident(u, w_out[None], 0, xs, mod, l, GATE_MIX, nxt_ffn, ROW_LATENT, tm_res)
            if need_ctx:
                u_c, _ = _conv_in(hc, sc_w_in, j, sc_conv, tm_c, tn=CTX_TILE)
                cs, hc = _out_proj_resident(u_c, w_out[None], 0, cs, mod, l, GATE_MIX, nxt_ffn, ROW_CTX, tm_c)
        a, w_down = _ffn_up(hx, ffn_w_up, l, ffn_conv, ffn_conv_b, tm_in, side_w=ffn_w_down, tn=FFN_COL_TILE)
        f_quarter = a.shape[1] // 4
        tk_x = f_quarter if f_quarter % HEAD_DIM == 0 else OUT_K_TILE
        xs, hx = _out_proj(a, w_down[None], 0, xs, mod, l, GATE_FFN, nxt_mix, ROW_LATENT, tm_res, tk=tk_x)
        if need_ctx:
            a_c, _ = _ffn_up(hc, ffn_w_up, l, ffn_conv, ffn_conv_b, tm_c, tn=CTX_TILE)
            f_half = a_c.shape[1] // 2
            tk_c = f_half if f_half % HEAD_DIM == 0 else CTX_TILE
            cs, hc = _out_proj(a_c, w_down[None], 0, cs, mod, l, GATE_FFN, nxt_mix, ROW_CTX, tm_c, tk=tk_c)
    return xs[None]
```
